```python
import math
import jax, jax.numpy as jnp
from jax import lax
import numpy as np

D_MODEL = 4096
BATCH = 16
SEQ = 256
DEPTH = 4
DEC_BATCH = 4
DEC_SEQ = 4096
PAST_LEN = 256

GRID_W = 64
N_MIXERS = 3
ALPHA = (2 * DEPTH) ** 0.25
BETA = (8 * DEPTH) ** -0.25
Q_BLOCK = 128
NA_HEADS = 32
NA_HEAD_DIM = D_MODEL // NA_HEADS
MAX_WIN_H = 8
WIN_W = 16
MLA_HEADS = 32
Q_LORA = D_MODEL // 4
KV_LORA = 512
QK_NOPE = 128
QK_ROPE = 64
V_DIM = 128
MLA_SCALE = (QK_NOPE + QK_ROPE) ** -0.5
ROPE_BASE = 10000.0
FNET_GROUPS = 8
FNET_GROUP_DIM = D_MODEL // FNET_GROUPS
D_FF = 256 * ((8 * D_MODEL // 3 + 255) // 256)
N_EXPERTS = 8
TOP_K = 2
D_FF_EXPERT = D_FF // 2
N_NA = (DEPTH + 2) // 3
N_MLA = (DEPTH + 1) // 3
N_FNET = DEPTH // 3
N_DENSE = (DEPTH + 1) // 2
N_MOE = DEPTH // 2

kernel_name = 'hybrid_na_mla_fourier_diffusion_step'


def layer_norm(x, g, b, eps=1e-5):
    xf = x.astype(jnp.float32)
    mu = jnp.mean(xf, axis=-1, keepdims=True)
    var = jnp.mean(jnp.square(xf - mu), axis=-1, keepdims=True)
    y = (xf - mu) * lax.rsqrt(var + eps)
    return (y * g.astype(jnp.float32) + b.astype(jnp.float32)).astype(x.dtype)


def rms_norm(x, g, eps=1e-6):
    xf = x.astype(jnp.float32)
    y = xf * lax.rsqrt(jnp.mean(jnp.square(xf), axis=-1, keepdims=True) + eps)
    return (y * g.astype(jnp.float32)).astype(x.dtype)


def adaln(cond, w, b):
    m = jax.nn.silu(cond) @ w + b
    return jnp.split(m[:, None, :], 6, axis=-1)


def modulate(h, shift, scale):
    return h * (1 + scale) + shift


def attend_blocked(q, k, v, scale):
    b, sq, h, dk = q.shape
    nb = sq // Q_BLOCK
    qb = jnp.moveaxis(q.reshape(b, nb, Q_BLOCK, h, dk), 1, 0)

    def one_block(q_blk):
        s = jnp.einsum('bqhd,bkhd->bhqk', q_blk, k).astype(jnp.float32) * scale
        p = jax.nn.softmax(s, axis=-1).astype(v.dtype)
        return jnp.einsum('bhqk,bkhd->bqhd', p, v)

    out = lax.map(one_block, qb)
    return jnp.moveaxis(out, 0, 1).reshape(b, sq, h, v.shape[-1])


def grid_rope_2d(x):
    n = x.shape[1]
    t = jnp.arange(n)
    row = (t // GRID_W).astype(jnp.float32)
    col = (t % GRID_W).astype(jnp.float32)
    half = QK_ROPE // 2
    nf = half // 2
    inv_freq = ROPE_BASE ** (-jnp.arange(nf, dtype=jnp.float32) / nf)

    def rot(xa, pos):
        ang = pos[:, None] * inv_freq[None, :]
        cos = jnp.cos(ang)[None, :, None, :].astype(x.dtype)
        sin = jnp.sin(ang)[None, :, None, :].astype(x.dtype)
        x1, x2 = xa[..., :nf], xa[..., nf:]
        return jnp.concatenate([x1 * cos - x2 * sin, x1 * sin + x2 * cos], axis=-1)

    return jnp.concatenate([rot(x[..., :half], row), rot(x[..., half:], col)], axis=-1)


def na_qkv(h, w_qkv):
    b, s, _ = h.shape
    qkv = (h @ w_qkv).reshape(b, s, 3, NA_HEADS, NA_HEAD_DIM)
    return qkv[:, :, 0], qkv[:, :, 1], qkv[:, :, 2]


def na_context(h, w_qkv, w_o):
    b, s, _ = h.shape
    q, k, v = na_qkv(h, w_qkv)
    o = attend_blocked(q, k, v, NA_HEAD_DIM ** -0.5)
    return o.reshape(b, s, D_MODEL) @ w_o, k, v


def na_latent(h, w_qkv, w_o, rel_bias, k_ctx, v_ctx):
    b, n, _ = h.shape
    rows = n // GRID_W
    win_h = min(MAX_WIN_H, rows)
    n_cb = GRID_W // WIN_W
    kw = 2 * WIN_W
    n_loc = win_h * kw
    scale = NA_HEAD_DIM ** -0.5
    q, k, v = na_qkv(h, w_qkv)
    q_grid = q.reshape(b, rows, GRID_W, NA_HEADS, NA_HEAD_DIM)
    k_grid = k.reshape(b, rows, GRID_W, NA_HEADS, NA_HEAD_DIM)
    v_grid = v.reshape(b, rows, GRID_W, NA_HEADS, NA_HEAD_DIM)
    q_col = np.arange(GRID_W).reshape(n_cb, WIN_W)
    q_cs = np.clip(q_col - WIN_W // 2, 0, GRID_W - WIN_W)
    k_cs = np.clip(np.arange(n_cb) * WIN_W - WIN_W // 2, 0, GRID_W - kw)
    k_col = k_cs[:, None] + np.arange(kw)[None, :]
    in_win = (k_col[:, None, :] >= q_cs[:, :, None]) & (k_col[:, None, :] < q_cs[:, :, None] + WIN_W)
    mask = jnp.asarray(np.broadcast_to(in_win[:, :, None, :], (n_cb, WIN_W, win_h, kw)).reshape(n_cb, WIN_W, n_loc))
    dc_idx = np.clip(k_col[:, None, :] - q_col[:, :, None] + WIN_W - 1, 0, 2 * WIN_W - 2)
    bias_tab = rel_bias.astype(jnp.float32)

    def row_block(args):
        r, q_row = args
        r0 = jnp.clip(r - win_h // 2, 0, rows - win_h)
        k_loc = lax.dynamic_slice_in_dim(k_grid, r0, win_h, axis=1)[:, :, k_col]
        v_loc = lax.dynamic_slice_in_dim(v_grid, r0, win_h, axis=1)[:, :, k_col]
        k_loc = jnp.swapaxes(k_loc, 1, 2).reshape(b, n_cb, n_loc, NA_HEADS, NA_HEAD_DIM)
        v_loc = jnp.swapaxes(v_loc, 1, 2).reshape(b, n_cb, n_loc, NA_HEADS, NA_HEAD_DIM)
        q_blk = q_row.reshape(b, n_cb, WIN_W, NA_HEADS, NA_HEAD_DIM)
        dr_idx = r0 + jnp.arange(win_h) - r + MAX_WIN_H - 1
        bias = bias_tab[:, dr_idx[None, None, :, None], dc_idx[:, :, None, :]].reshape(NA_HEADS, n_cb, WIN_W, n_loc)
        s_loc = jnp.einsum('bjqhd,bjkhd->bhjqk', q_blk, k_loc).astype(jnp.float32) * scale + bias
        s_loc = jnp.where(mask, s_loc, -jnp.inf)
        s_ctx = jnp.einsum('bjqhd,bkhd->bhjqk', q_blk, k_ctx).astype(jnp.float32) * scale
        p = jax.nn.softmax(jnp.concatenate([s_loc, s_ctx], axis=-1), axis=-1).astype(v.dtype)
        o = (jnp.einsum('bhjqk,bjkhd->bjqhd', p[..., :n_loc], v_loc)
             + jnp.einsum('bhjqk,bkhd->bjqhd', p[..., n_loc:], v_ctx))
        return o.reshape(b, GRID_W, NA_HEADS, NA_HEAD_DIM)

    o = lax.map(row_block, (jnp.arange(rows), jnp.moveaxis(q_grid, 1, 0)))
    o = jnp.moveaxis(o, 0, 1).reshape(b, n, D_MODEL)
    return o @ w_o


def mla_queries(h, w_dq, q_norm, w_uq):
    cq = rms_norm(h @ w_dq, q_norm)
    q = jnp.einsum('bsr,rhd->bshd', cq, w_uq)
    return q[..., :QK_NOPE], q[..., QK_NOPE:]


def mla_compress(h, w_dkv, kv_norm):
    kv = h @ w_dkv
    return rms_norm(kv[..., :KV_LORA], kv_norm), kv[..., KV_LORA:]


def mla_expand(ckv, kpe, w_uk, w_uv):
    k_nope = jnp.einsum('bsc,chd->bshd', ckv, w_uk)
    k_pe = jnp.broadcast_to(kpe[:, :, None, :], k_nope.shape[:3] + (QK_ROPE,))
    v = jnp.einsum('bsc,chd->bshd', ckv, w_uv)
    return jnp.concatenate([k_nope, k_pe], axis=-1), v


def mla_context(h, w_dq, q_norm, w_uq, w_dkv, kv_norm, w_uk, w_uv, w_o):
    b, s, _ = h.shape
    q_nope, q_pe = mla_queries(h, w_dq, q_norm, w_uq)
    ckv, kpe = mla_compress(h, w_dkv, kv_norm)
    k, v = mla_expand(ckv, kpe, w_uk, w_uv)
    o = attend_blocked(jnp.concatenate([q_nope, q_pe], axis=-1), k, v, MLA_SCALE)
    return o.reshape(b, s, MLA_HEADS * V_DIM) @ w_o, ckv, kpe


def mla_latent(h, w_dq, q_norm, w_uq, w_dkv, kv_norm, w_uk, w_uv, w_o, ckv_ctx, kpe_ctx):
    b, n, _ = h.shape
    q_nope, q_pe = mla_queries(h, w_dq, q_norm, w_uq)
    q = jnp.concatenate([q_nope, grid_rope_2d(q_pe)], axis=-1)
    ckv, kpe = mla_compress(h, w_dkv, kv_norm)
    kpe = grid_rope_2d(kpe[:, :, None, :])[:, :, 0]
    k_lat, v_lat = mla_expand(ckv, kpe, w_uk, w_uv)
    k_ctx, v_ctx = mla_expand(ckv_ctx, kpe_ctx, w_uk, w_uv)
    k = jnp.concatenate([k_lat, k_ctx], axis=1)
    v = jnp.concatenate([v_lat, v_ctx], axis=1)
    o = attend_blocked(q, k, v, MLA_SCALE)
    return o.reshape(b, n, MLA_HEADS * V_DIM) @ w_o


def fourier_mix(h, w_o):
    b, s, _ = h.shape
    hf = h.astype(jnp.float32).reshape(b, s, FNET_GROUPS, FNET_GROUP_DIM)
    mixed = jnp.fft.fft2(hf, axes=(1, 3), norm='ortho').real.astype(h.dtype)
    return mixed.reshape(b, s, D_MODEL) @ w_o


def swiglu(h, w_gate, w_up, w_down):
    return (jax.nn.silu(h @ w_gate) * (h @ w_up)) @ w_down


def moe_swiglu(h, w_router, b_router, w_gate, w_up, w_down):
    logits = (h @ w_router).astype(jnp.float32) + b_router.astype(jnp.float32)
    top_val, top_idx = lax.top_k(logits, TOP_K)
    gates = jax.nn.softmax(top_val, axis=-1)
    combine = jnp.einsum('bske,bsk->bse', jax.nn.one_hot(top_idx, N_EXPERTS, dtype=jnp.float32), gates).astype(h.dtype)
    out = jnp.zeros_like(h)
    for e in range(N_EXPERTS):
        out = out + combine[..., e:e + 1] * swiglu(h, w_gate[e], w_up[e], w_down[e])
    return out


def setup_inputs(seed: int = 0) -> dict:
    key = jax.random.key(seed)
    ks = iter(jax.random.split(key, 40))
    d = D_MODEL

    def nrm(shape, scale=1.0):
        return scale * jax.random.normal(next(ks), shape, dtype=jnp.float32)

    return {
        'x_prompt': nrm((BATCH, SEQ, d)),
        'x_sample': nrm((DEC_BATCH, DEC_SEQ, d)),
        'cache_na_k': nrm((DEC_BATCH, N_NA, PAST_LEN, NA_HEADS, NA_HEAD_DIM)),
        'cache_na_v': nrm((DEC_BATCH, N_NA, PAST_LEN, NA_HEADS, NA_HEAD_DIM)),
        'cache_mla_ckv': nrm((DEC_BATCH, N_MLA, PAST_LEN, KV_LORA)),
        'cache_mla_kpe': nrm((DEC_BATCH, N_MLA, PAST_LEN, QK_ROPE)),
        'c': nrm((DEC_BATCH, d)),
        'c_ctx': nrm((d,)),
        'w_ada': nrm((DEPTH, d, 6 * d), 0.5 * d ** -0.5),
        'b_ada': nrm((DEPTH, 6 * d), 0.02),
        'ln_g': 1.0 + nrm((DEPTH, 2, d), 0.02),
        'ln_b': nrm((DEPTH, 2, d), 0.02),
        'na_w_qkv': nrm((N_NA, d, 3 * d), d ** -0.5),
        'na_w_o': nrm((N_NA, d, d), BETA * d ** -0.5),
        'na_rel_bias': nrm((N_NA, NA_HEADS, 2 * MAX_WIN_H - 1, 2 * WIN_W - 1), 0.02),
        'mla_w_dq': nrm((N_MLA, d, Q_LORA), d ** -0.5),
        'mla_q_norm': 1.0 + nrm((N_MLA, Q_LORA), 0.02),
        'mla_w_uq': nrm((N_MLA, Q_LORA, MLA_HEADS, QK_NOPE + QK_ROPE), Q_LORA ** -0.5),
        'mla_w_dkv': nrm((N_MLA, d, KV_LORA + QK_ROPE), d ** -0.5),
        'mla_kv_norm': 1.0 + nrm((N_MLA, KV_LORA), 0.02),
        'mla_w_uk': nrm((N_MLA, KV_LORA, MLA_HEADS, QK_NOPE), KV_LORA ** -0.5),
        'mla_w_uv': nrm((N_MLA, KV_LORA, MLA_HEADS, V_DIM), KV_LORA ** -0.5),
        'mla_w_o': nrm((N_MLA, MLA_HEADS * V_DIM, d), BETA * (MLA_HEADS * V_DIM) ** -0.5),
        'fnet_w_o': nrm((N_FNET, d, d), BETA * d ** -0.5),
        'ffn_w_gate': nrm((N_DENSE, d, D_FF), d ** -0.5),
        'ffn_w_up': nrm((N_DENSE, d, D_FF), d ** -0.5),
        'ffn_w_down': nrm((N_DENSE, D_FF, d), BETA * D_FF ** -0.5),
        'moe_w_router': nrm((N_MOE, d, N_EXPERTS), d ** -0.5),
        'moe_b_router': nrm((N_MOE, N_EXPERTS), 0.01),
        'moe_w_gate': nrm((N_MOE, N_EXPERTS, d, D_FF_EXPERT), d ** -0.5),
        'moe_w_up': nrm((N_MOE, N_EXPERTS, d, D_FF_EXPERT), d ** -0.5),
        'moe_w_down': nrm((N_MOE, N_EXPERTS, D_FF_EXPERT, d), BETA * D_FF_EXPERT ** -0.5),
    }


def reference(x_prompt, x_sample, cache_na_k, cache_na_v, cache_mla_ckv, cache_mla_kpe, c, c_ctx,
              w_ada, b_ada, ln_g, ln_b, na_w_qkv, na_w_o, na_rel_bias,
              mla_w_dq, mla_q_norm, mla_w_uq, mla_w_dkv, mla_kv_norm, mla_w_uk, mla_w_uv, mla_w_o,
              fnet_w_o, ffn_w_gate, ffn_w_up, ffn_w_down,
              moe_w_router, moe_b_router, moe_w_gate, moe_w_up, moe_w_down):

    def channel_mixer(l, h):
        j = l // 2
        if l % 2 == 0:
            return swiglu(h, ffn_w_gate[j], ffn_w_up[j], ffn_w_down[j])
        return moe_swiglu(h, moe_w_router[j], moe_b_router[j], moe_w_gate[j], moe_w_up[j], moe_w_down[j])

    x = x_prompt
    na_k, na_v, mla_ckv, mla_kpe = [], [], [], []
    for l in range(DEPTH):
        sh1, sc1, g1, sh2, sc2, g2 = adaln(c_ctx[None, :], w_ada[l], b_ada[l])
        h = modulate(x, sh1, sc1)
        kind, j = l % N_MIXERS, l // N_MIXERS
        if kind == 0:
            y, k, v = na_context(h, na_w_qkv[j], na_w_o[j])
            na_k.append(k)
            na_v.append(v)
        elif kind == 1:
            y, ckv, kpe = mla_context(h, mla_w_dq[j], mla_q_norm[j], mla_w_uq[j], mla_w_dkv[j],
                                      mla_kv_norm[j], mla_w_uk[j], mla_w_uv[j], mla_w_o[j])
            mla_ckv.append(ckv)
            mla_kpe.append(kpe)
        else:
            y = fourier_mix(h, fnet_w_o[j])
        x = layer_norm(ALPHA * x + g1 * y, ln_g[l, 0], ln_b[l, 0])
        y = channel_mixer(l, modulate(x, sh2, sc2))
        x = layer_norm(ALPHA * x + g2 * y, ln_g[l, 1], ln_b[l, 1])
    y_prompt = x
    new_na_k = jnp.stack(na_k, axis=1)
    new_na_v = jnp.stack(na_v, axis=1)
    new_mla_ckv = jnp.stack(mla_ckv, axis=1)
    new_mla_kpe = jnp.stack(mla_kpe, axis=1)

    x = x_sample
    for l in range(DEPTH):
        sh1, sc1, g1, sh2, sc2, g2 = adaln(c, w_ada[l], b_ada[l])
        h = modulate(x, sh1, sc1)
        kind, j = l % N_MIXERS, l // N_MIXERS
        if kind == 0:
            y = na_latent(h, na_w_qkv[j], na_w_o[j], na_rel_bias[j], cache_na_k[:, j], cache_na_v[:, j])
        elif kind == 1:
            y = mla_latent(h, mla_w_dq[j], mla_q_norm[j], mla_w_uq[j], mla_w_dkv[j], mla_kv_norm[j],
                           mla_w_uk[j], mla_w_uv[j], mla_w_o[j], cache_mla_ckv[:, j], cache_mla_kpe[:, j])
        else:
            y = fourier_mix(h, fnet_w_o[j])
        x = layer_norm(ALPHA * x + g1 * y, ln_g[l, 0], ln_b[l, 0])
        y = channel_mixer(l, modulate(x, sh2, sc2))
        x = layer_norm(ALPHA * x + g2 * y, ln_g[l, 1], ln_b[l, 1])
    y_sample = x

    return (y_prompt, y_sample, new_na_k, new_na_v, new_mla_ckv, new_mla_kpe)
```

```python
import functools
import math

import numpy as np
import jax
import jax.numpy as jnp
from jax import lax
from jax.experimental import pallas as pl
from jax.experimental.pallas import tpu as pltpu

F32 = jnp.float32
BF16 = jnp.bfloat16

LANES = 128
VMEM_LIMIT = 56 * 1024 * 1024

GRID_W = 64
N_MIXERS = 3
TOP_K = 2
ROPE_BASE = 10000.0
LN_EPS = 1e-5
RMS_EPS = 1e-6
ROPE_SWAP = 16


def _tile(dim, pref, align):
    t = (min(pref, dim) // align) * align
    while t >= align:
        if dim % t == 0:
            return t
        t -= align
    return dim


def _params(sem):
    return pltpu.CompilerParams(dimension_semantics=sem, vmem_limit_bytes=VMEM_LIMIT)


def _bf16(v):
    return v if v.dtype == BF16 else v.astype(BF16)


def _mm_kernel(*refs, nk, n_b, n_extra, grouped, epilogue):
    if grouped:
        valid_ref = refs[1]
        refs = refs[2:]
    a_ref = refs[0]
    b_refs = refs[1:1 + n_b]
    extra_refs = refs[1 + n_b:1 + n_b + n_extra]
    o_ref = refs[1 + n_b + n_extra]
    acc_refs = refs[2 + n_b + n_extra:]

    def finish(accs):
        extras = [e[...] for e in extra_refs]
        o_ref[...] = epilogue(*accs, *extras).astype(o_ref.dtype)

    def compute():
        a = _bf16(a_ref[...])
        parts = [jnp.dot(a, b[...], preferred_element_type=F32) for b in b_refs]
        if nk == 1:
            finish(parts)
            return
        k = pl.program_id(2)

        @pl.when(k == 0)
        def _():
            for acc, p in zip(acc_refs, parts):
                acc[...] = p

        @pl.when(k > 0)
        def _():
            for acc, p in zip(acc_refs, parts):
                acc[...] += p

        @pl.when(k == nk - 1)
        def _():
            finish([acc[...] for acc in acc_refs])

    if not grouped:
        compute()
        return
    valid = valid_ref[pl.program_id(0)] > 0
    pl.when(valid)(compute)

    @pl.when(jnp.logical_and(jnp.logical_not(valid), pl.program_id(2) == nk - 1))
    def _():
        o_ref[...] = jnp.zeros(o_ref.shape, o_ref.dtype)


def _identity(acc):
    return acc


def _matmul(a, bs, *, out_dtype, tm, tn, tk, epilogue=_identity, extras=(), group=None, name="mm"):
    m, kdim = a.shape
    n = bs[0].shape[-1]
    tm, tn, tk = _tile(m, tm, 16), _tile(n, tn, LANES), _tile(kdim, tk, LANES)
    nk = kdim // tk
    grouped = group is not None
    if grouped:
        b_spec = pl.BlockSpec((None, tk, tn), lambda i, j, k, gid, valid: (gid[i], k, j))
    else:
        b_spec = pl.BlockSpec((tk, tn), lambda i, j, k, *_: (k, j))
    in_specs = [pl.BlockSpec((tm, tk), lambda i, j, k, *_: (i, k))]
    in_specs += [b_spec] * len(bs)
    in_specs += [pl.BlockSpec((1, tn), lambda i, j, k, *_: (0, j))] * len(extras)
    scratch = [pltpu.VMEM((tm, tn), F32) for _ in bs] if nk > 1 else []
    kern = functools.partial(_mm_kernel, nk=nk, n_b=len(bs), n_extra=len(extras), grouped=grouped,
                             epilogue=epilogue)
    call = pl.pallas_call(
        kern,
        out_shape=jax.ShapeDtypeStruct((m, n), out_dtype),
        grid_spec=pltpu.PrefetchScalarGridSpec(
            num_scalar_prefetch=2 if grouped else 0,
            grid=(m // tm, n // tn, nk),
            in_specs=in_specs,
            out_specs=pl.BlockSpec((tm, tn), lambda i, j, k, *_: (i, j)),
            scratch_shapes=scratch),
        compiler_params=_params(("parallel", "parallel", "arbitrary")),
        name=name)
    prefetch = tuple(group) if grouped else ()
    return call(*prefetch, a, *bs, *extras)


def _swiglu_epilogue(g, u):
    return jax.nn.silu(g) * u


def _rms_epilogue(n_norm, acc, gain):
    width = acc.shape[-1]
    if n_norm == width:
        ms = jnp.mean(jnp.square(acc), axis=-1, keepdims=True)
        return acc * lax.rsqrt(ms + RMS_EPS) * gain
    lane = lax.broadcasted_iota(jnp.int32, acc.shape, 1)
    head = lane < n_norm
    ms = jnp.sum(jnp.where(head, jnp.square(acc), 0.0), axis=-1, keepdims=True) / n_norm
    return jnp.where(head, acc * lax.rsqrt(ms + RMS_EPS) * gain, acc)


def _ada_kernel(c_ref, w_ref, b_ref, o_ref):
    s = jax.nn.silu(c_ref[...]).astype(BF16)
    o_ref[...] = jnp.dot(s, w_ref[...].astype(BF16), preferred_element_type=F32) + b_ref[...]


def _adaln(cond, w_ada, b_ada):
    depth, d, n = w_ada.shape
    g = cond.shape[0]
    tn = _tile(n, 512, LANES)
    return pl.pallas_call(
        _ada_kernel,
        out_shape=jax.ShapeDtypeStruct((depth, g, n), F32),
        grid=(depth, n // tn),
        in_specs=[pl.BlockSpec((g, d), lambda l, j: (0, 0)),
                  pl.BlockSpec((None, d, tn), lambda l, j: (l, 0, j)),
                  pl.BlockSpec((None, 1, tn), lambda l, j: (l, 0, j))],
        out_specs=pl.BlockSpec((None, g, tn), lambda l, j: (l, 0, j)),
        compiler_params=_params(("parallel", "parallel")),
        name="adaln")(cond, w_ada, b_ada.reshape(depth, 1, n))


def _modulate_kernel(gid_ref, x_ref, sh_ref, sc_ref, h_ref):
    h_ref[...] = (x_ref[...] * (1.0 + sc_ref[...]) + sh_ref[...]).astype(h_ref.dtype)


def _modulate(x, shift, scale, tile_group, tm, out_dtype):
    t, d = x.shape
    row = pl.BlockSpec((tm, d), lambda i, gid: (i, 0))
    mod = pl.BlockSpec((None, 1, d), lambda i, gid: (gid[i], 0, 0))
    return pl.pallas_call(
        _modulate_kernel,
        out_shape=jax.ShapeDtypeStruct((t, d), out_dtype),
        grid_spec=pltpu.PrefetchScalarGridSpec(
            num_scalar_prefetch=1, grid=(t // tm,), in_specs=[row, mod, mod], out_specs=row),
        compiler_params=_params(("parallel",)),
        name="modulate")(tile_group, x, shift, scale)


def _ln_kernel(gid_ref, *refs, alpha, n_y, emit_h):
    x_ref = refs[0]
    if n_y == 1:
        y = refs[1][...]
        refs = refs[2:]
    else:
        gates = refs[3][...]
        y = gates[:, 0:1] * refs[1][...] + gates[:, 1:2] * refs[2][...]
        refs = refs[4:]
    g_ref, lg_ref, lb_ref = refs[:3]
    z = alpha * x_ref[...] + g_ref[...] * y
    mu = jnp.mean(z, axis=-1, keepdims=True)
    zc = z - mu
    var = jnp.mean(jnp.square(zc), axis=-1, keepdims=True)
    xn = zc * lax.rsqrt(var + LN_EPS) * lg_ref[...] + lb_ref[...]
    if emit_h:
        sh_ref, sc_ref, xo_ref, ho_ref = refs[3:]
        xo_ref[...] = xn
        ho_ref[...] = (xn * (1.0 + sc_ref[...]) + sh_ref[...]).astype(ho_ref.dtype)
    else:
        refs[3][...] = xn


def _residual_ln(x, ys, gate, ln_g, ln_b, nxt, tile_group, tm, alpha, h_dtype=BF16, gates=None):
    t, d = x.shape
    row = pl.BlockSpec((tm, d), lambda i, gid: (i, 0))
    mod = pl.BlockSpec((None, 1, d), lambda i, gid: (gid[i], 0, 0))
    vec = pl.BlockSpec((1, d), lambda i, gid: (0, 0))
    args = [x, *ys]
    in_specs = [row] * len(args)
    if gates is not None:
        args.append(gates)
        in_specs.append(pl.BlockSpec((tm, LANES), lambda i, gid: (i, 0)))
    args += [gate, ln_g.reshape(1, d), ln_b.reshape(1, d)]
    in_specs += [mod, vec, vec]
    out_shape = [jax.ShapeDtypeStruct((t, d), F32)]
    out_specs = [row]
    if nxt is not None:
        args += list(nxt)
        in_specs += [mod, mod]
        out_shape.append(jax.ShapeDtypeStruct((t, d), h_dtype))
        out_specs.append(row)
    kern = functools.partial(_ln_kernel, alpha=alpha, n_y=len(ys), emit_h=nxt is not None)
    out = pl.pallas_call(
        kern,
        out_shape=out_shape,
        grid_spec=pltpu.PrefetchScalarGridSpec(
            num_scalar_prefetch=1, grid=(t // tm,), in_specs=in_specs, out_specs=out_specs),
        compiler_params=_params(("parallel",)),
        name="residual_ln")(tile_group, *args)
    return (out[0], out[1]) if nxt is not None else (out[0], None)


def _rope(x, cos, sin_up, sin_dn):
    w = x.shape[-1]
    return x * cos + pltpu.roll(x, w - ROPE_SWAP, 1) * sin_up + pltpu.roll(x, ROPE_SWAP, 1) * sin_dn


def _attn_kernel(*refs, scale, sk, ck, rope):
    if rope:
        q_ref, k_ref, v_ref, cos_ref, su_ref, sd_ref, o_ref = refs
        q = _rope(q_ref[...], cos_ref[...], su_ref[...], sd_ref[...])
    else:
        q_ref, k_ref, v_ref, o_ref = refs
        q = q_ref[...]
    q = (q * scale).astype(BF16)
    tq = q.shape[0]
    dv = v_ref.shape[-1]
    dims = (((1,), (1,)), ((), ()))

    def scores(start):
        k = _bf16(k_ref[pl.ds(start, ck), :])
        return lax.dot_general(q, k, dims, preferred_element_type=F32)

    def pv(p, start):
        return jnp.dot(p.astype(BF16), _bf16(v_ref[pl.ds(start, ck), :]), preferred_element_type=F32)

    if sk == ck:
        s = scores(0)
        p = jnp.exp(s - jnp.max(s, axis=-1, keepdims=True))
        o_ref[...] = (pv(p, 0) / jnp.sum(p, axis=-1, keepdims=True)).astype(o_ref.dtype)
        return

    def step(c, carry):
        m, l, acc = carry
        start = pl.multiple_of(c * ck, ck)
        s = scores(start)
        m_new = jnp.maximum(m, jnp.max(s, axis=-1, keepdims=True))
        a = jnp.exp(m - m_new)
        p = jnp.exp(s - m_new)
        return m_new, a * l + jnp.sum(p, axis=-1, keepdims=True), a * acc + pv(p, start)

    init = (jnp.full((tq, 1), -jnp.inf, F32), jnp.zeros((tq, 1), F32), jnp.zeros((tq, dv), F32))
    _, l, acc = lax.fori_loop(0, sk // ck, step, init)
    o_ref[...] = (acc / l).astype(o_ref.dtype)


def _attention(q_arr, k_arr, v_arr, *, nb, nh, sq, sk, dk, dv, scale, q_row0=0, k_row0=0,
               q_col0=0, k_col0=0, v_col0=0, rope_tabs=None, name="attention"):
    tq = _tile(sq, 512, 16)
    ck = _tile(sk, 512, LANES)
    assert q_row0 % tq == 0 and k_row0 % sk == 0
    nq = sq // tq
    q_spec = pl.BlockSpec((tq, dk), lambda b, h, i: (q_row0 // tq + b * nq + i, q_col0 + h))
    k_spec = pl.BlockSpec((sk, dk), lambda b, h, i: (k_row0 // sk + b, k_col0 + h))
    v_spec = pl.BlockSpec((sk, dv), lambda b, h, i: (k_row0 // sk + b, v_col0 + h))
    args, in_specs = [q_arr, k_arr, v_arr], [q_spec, k_spec, v_spec]
    if rope_tabs is not None:
        args += list(rope_tabs)
        in_specs += [pl.BlockSpec((tq, dk), lambda b, h, i: (i, 0))] * 3
    kern = functools.partial(_attn_kernel, scale=scale, sk=sk, ck=ck, rope=rope_tabs is not None)
    return pl.pallas_call(
        kern,
        out_shape=jax.ShapeDtypeStruct((nb * sq, nh * dv), BF16),
        grid=(nb, nh, nq),
        in_specs=in_specs,
        out_specs=pl.BlockSpec((tq, dv), lambda b, h, i: (b * nq + i, h)),
        compiler_params=_params(("parallel", "parallel", "parallel")),
        name=name)(*args)


def _na_kernel(q_ref, k_ref, v_ref, kc_ref, vc_ref, tab_ref, o_ref, kb_ref, vb_ref, *, rows, win_h, scale):
    kb_ref[...] = k_ref[...].astype(BF16)
    vb_ref[...] = v_ref[...].astype(BF16)
    kc = _bf16(kc_ref[...])
    vc = _bf16(vc_ref[...])
    n_loc = win_h * GRID_W
    dims = (((1,), (1,)), ((), ()))

    def row(r, carry):
        r0 = jnp.clip(r - win_h // 2, 0, rows - win_h)
        q0 = pl.multiple_of(r * GRID_W, GRID_W)
        k0 = pl.multiple_of(r0 * GRID_W, GRID_W)
        q = (q_ref[pl.ds(q0, GRID_W), :] * scale).astype(BF16)
        s_loc = lax.dot_general(q, kb_ref[pl.ds(k0, n_loc), :], dims, preferred_element_type=F32) + tab_ref[r - r0]
        s_ctx = lax.dot_general(q, kc, dims, preferred_element_type=F32)
        m = jnp.maximum(jnp.max(s_loc, axis=-1, keepdims=True), jnp.max(s_ctx, axis=-1, keepdims=True))
        p_loc = jnp.exp(s_loc - m)
        p_ctx = jnp.exp(s_ctx - m)
        l = jnp.sum(p_loc, axis=-1, keepdims=True) + jnp.sum(p_ctx, axis=-1, keepdims=True)
        o = (jnp.dot(p_loc.astype(BF16), vb_ref[pl.ds(k0, n_loc), :], preferred_element_type=F32)
             + jnp.dot(p_ctx.astype(BF16), vc, preferred_element_type=F32))
        o_ref[pl.ds(q0, GRID_W), :] = (o / l).astype(o_ref.dtype)
        return carry

    lax.fori_loop(0, rows, row, 0)


def _na_bias_table(rel_bias, win_h):
    nh, n_dr, n_dc = rel_bias.shape
    max_win_h, win_w = (n_dr + 1) // 2, (n_dc + 1) // 2
    qc = np.arange(GRID_W)
    cs = np.clip(qc - win_w // 2, 0, GRID_W - win_w)
    kcol = np.arange(GRID_W)
    in_win = (kcol[None, :] >= cs[:, None]) & (kcol[None, :] < cs[:, None] + win_w)
    dc = np.clip(kcol[None, :] - qc[:, None] + win_w - 1, 0, n_dc - 1)
    dr = np.arange(win_h)[None, :] - np.arange(win_h)[:, None] + max_win_h - 1
    tab = rel_bias.astype(F32)[:, dr[:, None, :, None], dc[None, :, None, :]]
    tab = jnp.where(jnp.asarray(in_win)[None, None, :, None, :], tab, -jnp.inf)
    return tab.reshape(nh, win_h, GRID_W, win_h * GRID_W)


def _na_latent(qkv, row0, k_ctx, v_ctx, rel_bias, *, nb, n, nh, hd):
    rows = n // GRID_W
    win_h = min((rel_bias.shape[1] + 1) // 2, rows)
    n_loc = win_h * GRID_W
    assert row0 % n == 0
    blk0 = row0 // n
    p = k_ctx.shape[0] // nb
    tab = _na_bias_table(rel_bias, win_h)
    kern = functools.partial(_na_kernel, rows=rows, win_h=win_h, scale=hd ** -0.5)
    return pl.pallas_call(
        kern,
        out_shape=jax.ShapeDtypeStruct((nb * n, nh * hd), BF16),
        grid=(nb, nh),
        in_specs=[pl.BlockSpec((n, hd), lambda b, h: (blk0 + b, h)),
                  pl.BlockSpec((n, hd), lambda b, h: (blk0 + b, nh + h)),
                  pl.BlockSpec((n, hd), lambda b, h: (blk0 + b, 2 * nh + h)),
                  pl.BlockSpec((p, hd), lambda b, h: (b, h)),
                  pl.BlockSpec((p, hd), lambda b, h: (b, h)),
                  pl.BlockSpec((None, win_h, GRID_W, n_loc), lambda b, h: (h, 0, 0, 0))],
        out_specs=pl.BlockSpec((n, hd), lambda b, h: (b, h)),
        scratch_shapes=[pltpu.VMEM((n, hd), BF16), pltpu.VMEM((n, hd), BF16)],
        compiler_params=_params(("parallel", "parallel")),
        name="na_latent")(qkv, qkv, qkv, k_ctx, v_ctx, tab)


def _rope_kernel(x_ref, cos_ref, su_ref, sd_ref, o_ref):
    o_ref[...] = _rope(x_ref[...], cos_ref[...], su_ref[...], sd_ref[...])


def _rope_rows(x, col_blk, row0, nb, n, tabs):
    tr = _tile(n, 1024, 8)
    assert row0 % tr == 0
    nt = n // tr
    return pl.pallas_call(
        _rope_kernel,
        out_shape=jax.ShapeDtypeStruct((nb * n, LANES), F32),
        grid=(nb, nt),
        in_specs=[pl.BlockSpec((tr, LANES), lambda b, i: (row0 // tr + b * nt + i, col_blk))]
        + [pl.BlockSpec((tr, LANES), lambda b, i: (i, 0))] * 3,
        out_specs=pl.BlockSpec((tr, LANES), lambda b, i: (b * nt + i, 0)),
        compiler_params=_params(("parallel", "parallel")),
        name="rope_kpe")(x, *tabs)


def _rope_tables(n, width, lane0, rope_dim):
    t = jnp.arange(n)
    row = (t // GRID_W).astype(F32)
    col = (t % GRID_W).astype(F32)
    half = rope_dim // 2
    nf = half // 2
    assert nf == ROPE_SWAP
    inv_freq = ROPE_BASE ** (-jnp.arange(nf, dtype=F32) / nf)
    ang_r = row[:, None] * inv_freq[None, :]
    ang_c = col[:, None] * inv_freq[None, :]
    zeros = jnp.zeros((n, nf), F32)
    cos = jnp.concatenate([jnp.cos(ang_r)] * 2 + [jnp.cos(ang_c)] * 2, axis=1)
    sin_up = jnp.concatenate([-jnp.sin(ang_r), zeros, -jnp.sin(ang_c), zeros], axis=1)
    sin_dn = jnp.concatenate([zeros, jnp.sin(ang_r), zeros, jnp.sin(ang_c)], axis=1)

    def place(tab, fill):
        left = jnp.full((n, lane0), fill, F32)
        right = jnp.full((n, width - lane0 - rope_dim), fill, F32)
        return jnp.concatenate([left, tab, right], axis=1)

    return place(cos, 1.0), place(sin_up, 0.0), place(sin_dn, 0.0)


def _dft_feat_kernel(h_ref, c_ref, s_ref, yc_ref, ys_ref):
    h = _bf16(h_ref[...])
    yc_ref[...] = jnp.dot(h, c_ref[...], preferred_element_type=F32).astype(yc_ref.dtype)
    ys_ref[...] = jnp.dot(h, s_ref[...], preferred_element_type=F32).astype(ys_ref.dtype)


def _dft_seq_kernel(c_ref, s_ref, yc_ref, ys_ref, o_ref, acc_ref, *, nk):
    k = pl.program_id(3)
    part = (jnp.dot(c_ref[...], yc_ref[...], preferred_element_type=F32)
            - jnp.dot(s_ref[...], ys_ref[...], preferred_element_type=F32))

    @pl.when(k == 0)
    def _():
        acc_ref[...] = part

    @pl.when(k > 0)
    def _():
        acc_ref[...] += part

    @pl.when(k == nk - 1)
    def _():
        o_ref[...] = acc_ref[...].astype(o_ref.dtype)


def _dft_mats(n):
    j = jnp.arange(n, dtype=jnp.int32)
    ang = ((j[:, None] * j[None, :]) % n).astype(F32) * (2.0 * math.pi / n)
    return (jnp.cos(ang) * n ** -0.5).astype(BF16), (jnp.sin(ang) * n ** -0.5).astype(BF16)


def _fourier_mix(h, segments, n_groups):
    t, d = h.shape
    dg = d // n_groups
    cd, sd = _dft_mats(dg)
    tm = _tile(t, 1024, 16)
    row = pl.BlockSpec((tm, dg), lambda i, g: (i, g))
    mat = pl.BlockSpec((dg, dg), lambda i, g: (0, 0))
    yc, ys = pl.pallas_call(
        _dft_feat_kernel,
        out_shape=[jax.ShapeDtypeStruct((t, d), BF16)] * 2,
        grid=(t // tm, n_groups),
        in_specs=[row, mat, mat],
        out_specs=[row, row],
        compiler_params=_params(("parallel", "parallel")),
        name="dft_features")(h, cd, sd)
    outs = []
    for row0, nb, s in segments:
        cs, ss = _dft_mats(s)
        tms, tks, tn = _tile(s, 1024, 16), _tile(s, 1024, LANES), _tile(d, 1024, LANES)
        assert row0 % tms == 0 and row0 % tks == 0
        ni, nk = s // tms, s // tks
        a_spec = pl.BlockSpec((tms, tks), lambda b, i, j, k: (i, k))
        y_spec = pl.BlockSpec((tks, tn), lambda b, i, j, k: (row0 // tks + b * nk + k, j))
        outs.append(pl.pallas_call(
            functools.partial(_dft_seq_kernel, nk=nk),
            out_shape=jax.ShapeDtypeStruct((nb * s, d), BF16),
            grid=(nb, ni, d // tn, nk),
            in_specs=[a_spec, a_spec, y_spec, y_spec],
            out_specs=pl.BlockSpec((tms, tn), lambda b, i, j, k: (b * ni + i, j)),
            scratch_shapes=[pltpu.VMEM((tms, tn), F32)],
            compiler_params=_params(("parallel", "parallel", "parallel", "arbitrary")),
            name="dft_tokens")(cs, ss, yc, ys))
    return jnp.concatenate(outs, axis=0)


def _router_kernel(h_ref, w_ref, b_ref, idx_ref, gate_ref):
    logits = jnp.dot(_bf16(h_ref[...]), w_ref[...], preferred_element_type=F32) + b_ref[...]
    lane = lax.broadcasted_iota(jnp.int32, logits.shape, 1)
    v1 = jnp.max(logits, axis=-1, keepdims=True)
    i1 = jnp.min(jnp.where(logits == v1, lane, LANES), axis=-1, keepdims=True)
    rest = jnp.where(lane == i1, -jnp.inf, logits)
    v2 = jnp.max(rest, axis=-1, keepdims=True)
    i2 = jnp.min(jnp.where(rest == v2, lane, LANES), axis=-1, keepdims=True)
    e2 = jnp.exp(v2 - v1)
    g1 = 1.0 / (1.0 + e2)
    g2 = e2 / (1.0 + e2)
    idx_ref[...] = jnp.where(lane == 0, i1, jnp.where(lane == 1, i2, 0))
    gate_ref[...] = jnp.where(lane == 0, g1, jnp.where(lane == 1, g2, 0.0))


def _router(h, w_router, b_router, tm):
    t, d = h.shape
    ne = w_router.shape[1]
    w = jnp.pad(w_router, ((0, 0), (0, LANES - ne))).astype(BF16)
    b = jnp.concatenate([b_router.astype(F32), jnp.full((LANES - ne,), -jnp.inf, F32)]).reshape(1, LANES)
    out = pl.BlockSpec((tm, LANES), lambda i: (i, 0))
    return pl.pallas_call(
        _router_kernel,
        out_shape=[jax.ShapeDtypeStruct((t, LANES), jnp.int32), jax.ShapeDtypeStruct((t, LANES), F32)],
        grid=(t // tm,),
        in_specs=[pl.BlockSpec((tm, d), lambda i: (i, 0)),
                  pl.BlockSpec((d, LANES), lambda i: (0, 0)),
                  pl.BlockSpec((1, LANES), lambda i: (0, 0))],
        out_specs=[out, out],
        compiler_params=_params(("parallel",)),
        name="router")(h, w, b)


def _gather_kernel(idx_ref, src_ref, o_ref, sem, *, rows):
    base = pl.program_id(0) * rows

    def copy(r, src_row):
        return pltpu.make_async_copy(src_ref.at[pl.ds(src_row, 1), :], o_ref.at[pl.ds(r, 1), :], sem)

    def issue(r, carry):
        copy(r, idx_ref[base + r]).start()
        return carry

    def drain(r, carry):
        copy(r, 0).wait()
        return carry

    lax.fori_loop(0, rows, issue, 0)
    lax.fori_loop(0, rows, drain, 0)


def _gather_rows(src, idx, rows):
    p = idx.shape[0]
    d = src.shape[1]
    rows = _tile(p, rows, 8)
    return pl.pallas_call(
        functools.partial(_gather_kernel, rows=rows),
        out_shape=jax.ShapeDtypeStruct((p, d), src.dtype),
        grid_spec=pltpu.PrefetchScalarGridSpec(
            num_scalar_prefetch=1, grid=(p // rows,),
            in_specs=[pl.BlockSpec(memory_space=pl.ANY)],
            out_specs=pl.BlockSpec((rows, d), lambda i, idx: (i, 0)),
            scratch_shapes=[pltpu.SemaphoreType.DMA]),
        compiler_params=_params(("arbitrary",)),
        name="gather_rows")(idx, src)


def _dispatch_plan(top_idx, n_experts, tm):
    t, k = top_idx.shape
    flat = top_idx.reshape(-1)
    onehot = (flat[:, None] == jnp.arange(n_experts, dtype=jnp.int32)[None, :]).astype(jnp.int32)
    rank = jnp.sum((jnp.cumsum(onehot, axis=0) - 1) * onehot, axis=1)
    counts = jnp.sum(onehot, axis=0)
    padded = ((counts + tm - 1) // tm) * tm
    ends = jnp.cumsum(padded)
    starts = ends - padded
    pos = starts[flat] + rank
    p_total = ((t * k + n_experts * (tm - 1)) // tm) * tm
    src_token = jnp.zeros((p_total,), jnp.int32).at[pos].set(jnp.arange(t * k, dtype=jnp.int32) // k)
    tile_start = jnp.arange(p_total // tm, dtype=jnp.int32) * tm
    tile_expert = jnp.minimum(jnp.sum((tile_start[:, None] >= ends[None, :]).astype(jnp.int32), axis=1),
                              n_experts - 1).astype(jnp.int32)
    tile_valid = (tile_start < ends[-1]).astype(jnp.int32)
    return src_token, pos.reshape(t, k).astype(jnp.int32), tile_expert, tile_valid


def _pad_cols(w, n):
    return jnp.pad(w, [(0, 0)] * (w.ndim - 1) + [(0, n - w.shape[-1])])


def _pad_rows(w, n):
    return jnp.pad(w, [(0, 0)] * (w.ndim - 2) + [(0, n - w.shape[-2]), (0, 0)])


def _round_up(n, m):
    return ((n + m - 1) // m) * m


def kernel(x_prompt, x_sample, cache_na_k, cache_na_v, cache_mla_ckv, cache_mla_kpe, c, c_ctx, w_ada, b_ada, ln_g, ln_b, na_w_qkv, na_w_o, na_rel_bias, mla_w_dq, mla_q_norm, mla_w_uq, mla_w_dkv, mla_kv_norm, mla_w_uk, mla_w_uv, mla_w_o, fnet_w_o, ffn_w_gate, ffn_w_up, ffn_w_down, moe_w_router, moe_b_router, moe_w_gate, moe_w_up, moe_w_down):
    nbc, seq, d = x_prompt.shape
    nbl, n_lat, _ = x_sample.shape
    depth = w_ada.shape[0]
    alpha = (2 * depth) ** 0.25
    tc, tl = nbc * seq, nbl * n_lat
    t = tc + tl
    past = cache_na_k.shape[2]
    na_heads, na_hd = cache_na_k.shape[3], cache_na_k.shape[4]
    kv_lora, qk_rope = cache_mla_ckv.shape[-1], cache_mla_kpe.shape[-1]
    mla_heads, qk_dim = mla_w_uq.shape[2], mla_w_uq.shape[3]
    qk_nope = qk_dim - qk_rope
    v_dim = mla_w_uv.shape[-1]
    n_experts = moe_w_router.shape[-1]
    fnet_groups = 8
    assert qk_nope == LANES and v_dim == LANES and na_hd == LANES and qk_rope <= LANES // 2
    hq = 2 * LANES

    tm_tok = _tile(math.gcd(tc, n_lat), 128, 8)
    tile_group = jnp.asarray([0 if i * tm_tok < tc else 1 + (i * tm_tok - tc) // n_lat
                              for i in range(t // tm_tok)], jnp.int32)

    x = jnp.concatenate([x_prompt.reshape(tc, d), x_sample.reshape(tl, d)], axis=0)
    n_cond = 1 + nbl
    cond = jnp.concatenate([c_ctx[None, :], c, jnp.zeros((_round_up(n_cond, 8) - n_cond, d), F32)], axis=0)
    mod = _adaln(cond, w_ada, b_ada)
    mod = mod.reshape(depth, cond.shape[0], 6, 1, d).transpose(0, 2, 1, 3, 4)

    h = _modulate(x, mod[0, 0], mod[0, 1], tile_group, tm_tok, BF16)
    na_k, na_v, mla_ckv, mla_kpe = [], [], [], []

    for l in range(depth):
        kind, j = l % N_MIXERS, l // N_MIXERS
        if kind == 0:
            qkv = _matmul(h, [na_w_qkv[j].astype(BF16)], out_dtype=F32, tm=1024, tn=1024, tk=4096, name="na_qkv")
            na_k.append(qkv[:tc, d:2 * d].reshape(nbc, seq, na_heads, na_hd))
            na_v.append(qkv[:tc, 2 * d:].reshape(nbc, seq, na_heads, na_hd))
            o_ctx = _attention(qkv, qkv, qkv, nb=nbc, nh=na_heads, sq=seq, sk=seq, dk=na_hd, dv=na_hd,
                               scale=na_hd ** -0.5, k_col0=na_heads, v_col0=2 * na_heads, name="na_context")
            o_lat = _na_latent(qkv, tc, cache_na_k[:, j].reshape(nbl * past, d),
                               cache_na_v[:, j].reshape(nbl * past, d), na_rel_bias[j],
                               nb=nbl, n=n_lat, nh=na_heads, hd=na_hd)
            w_o = na_w_o[j]
        elif kind == 1:
            scale = qk_dim ** -0.5
            cq = _matmul(h, [mla_w_dq[j].astype(BF16)], out_dtype=BF16, tm=1024, tn=mla_w_dq.shape[-1], tk=4096,
                         epilogue=functools.partial(_rms_epilogue, mla_w_dq.shape[-1]),
                         extras=[mla_q_norm[j].reshape(1, -1)], name="mla_dq")
            w_uq = _pad_cols(mla_w_uq[j], hq).reshape(-1, mla_heads * hq).astype(BF16)
            q = _matmul(cq, [w_uq], out_dtype=F32, tm=1024, tn=1024, tk=4096, name="mla_uq")
            kvw = _round_up(kv_lora + qk_rope, LANES)
            kv = _matmul(h, [_pad_cols(mla_w_dkv[j], kvw).astype(BF16)], out_dtype=F32, tm=1024, tn=kvw, tk=4096,
                         epilogue=functools.partial(_rms_epilogue, kv_lora),
                         extras=[_pad_cols(mla_kv_norm[j].reshape(1, -1), kvw)], name="mla_dkv")
            mla_ckv.append(kv[:tc, :kv_lora].reshape(nbc, seq, kv_lora))
            mla_kpe.append(kv[:tc, kv_lora:kv_lora + qk_rope].reshape(nbc, seq, qk_rope))
            kpe_lat = _rope_rows(kv, kv_lora // LANES, tc, nbl, n_lat, _rope_tables(n_lat, LANES, 0, qk_rope))
            ckv_lat = jnp.concatenate([kv[tc:, :kv_lora], kpe_lat], axis=1).reshape(nbl, n_lat, kvw)
            ckv_past = _pad_cols(jnp.concatenate([cache_mla_ckv[:, j], cache_mla_kpe[:, j]], axis=-1), kvw)
            ckv_all = jnp.concatenate([ckv_lat, ckv_past], axis=1).reshape(nbl * (n_lat + past), kvw).astype(BF16)
            ckv_ctx = kv[:tc].astype(BF16)
            eye = jnp.pad(jnp.eye(qk_rope, dtype=F32), ((0, kvw - kv_lora - qk_rope), (0, 0)))
            w_k = jnp.concatenate([
                jnp.concatenate([mla_w_uk[j], jnp.zeros((kvw - kv_lora, mla_heads, qk_nope), F32)], axis=0),
                jnp.broadcast_to(jnp.concatenate([jnp.zeros((kv_lora, qk_rope), F32), eye], axis=0)[:, None, :],
                                 (kvw, mla_heads, qk_rope)),
                jnp.zeros((kvw, mla_heads, hq - qk_dim), F32)], axis=-1).reshape(kvw, mla_heads * hq).astype(BF16)
            w_v = _pad_rows(mla_w_uv[j].reshape(kv_lora, mla_heads * v_dim), kvw).astype(BF16)
            k_ctx = _matmul(ckv_ctx, [w_k], out_dtype=BF16, tm=1024, tn=1024, tk=kvw, name="mla_k_ctx")
            v_ctx = _matmul(ckv_ctx, [w_v], out_dtype=BF16, tm=1024, tn=1024, tk=kvw, name="mla_v_ctx")
            k_lat = _matmul(ckv_all, [w_k], out_dtype=BF16, tm=1024, tn=1024, tk=kvw, name="mla_k_lat")
            v_lat = _matmul(ckv_all, [w_v], out_dtype=BF16, tm=1024, tn=1024, tk=kvw, name="mla_v_lat")
            o_ctx = _attention(q, k_ctx, v_ctx, nb=nbc, nh=mla_heads, sq=seq, sk=seq, dk=hq, dv=v_dim,
                               scale=scale, name="mla_context")
            o_lat = _attention(q, k_lat, v_lat, nb=nbl, nh=mla_heads, sq=n_lat, sk=n_lat + past, dk=hq, dv=v_dim,
                               scale=scale, q_row0=tc, rope_tabs=_rope_tables(n_lat, hq, qk_nope, qk_rope),
                               name="mla_latent")
            w_o = mla_w_o[j]
        else:
            o_ctx = None
            o_lat = _fourier_mix(h, [(0, nbc, seq), (tc, nbl, n_lat)], fnet_groups)
            w_o = fnet_w_o[j]
        o = o_lat if o_ctx is None else jnp.concatenate([o_ctx, o_lat], axis=0)
        y = _matmul(o, [w_o.astype(BF16)], out_dtype=F32, tm=1024, tn=1024, tk=4096, name="mixer_out")

        jc = l // 2
        moe = l % 2 == 1
        x, h = _residual_ln(x, [y], mod[l, 2], ln_g[l, 0], ln_b[l, 0], (mod[l, 3], mod[l, 4]), tile_group, tm_tok,
                            alpha, h_dtype=F32 if moe else BF16)
        nxt = (mod[l + 1, 0], mod[l + 1, 1]) if l + 1 < depth else None
        if not moe:
            dff = _round_up(ffn_w_gate.shape[-1], 1024)
            u = _matmul(h, [_pad_cols(ffn_w_gate[jc], dff).astype(BF16), _pad_cols(ffn_w_up[jc], dff).astype(BF16)],
                        out_dtype=BF16, tm=1024, tn=512, tk=4096, epilogue=_swiglu_epilogue, name="ffn_up")
            y = _matmul(u, [_pad_rows(ffn_w_down[jc], dff).astype(BF16)], out_dtype=F32, tm=1024, tn=1024, tk=2816,
                        name="ffn_down")
            x, h = _residual_ln(x, [y], mod[l, 5], ln_g[l, 1], ln_b[l, 1], nxt, tile_group, tm_tok, alpha)
        else:
            tm_e = 512
            dfe = _round_up(moe_w_gate.shape[-1], 512)
            idx, gates = _router(h, moe_w_router[jc], moe_b_router[jc], tm_tok)
            src_token, slot_pos, tile_expert, tile_valid = _dispatch_plan(idx[:, :TOP_K], n_experts, tm_e)
            hs = _gather_rows(h, src_token, 256)
            group = (tile_expert, tile_valid)
            u = _matmul(hs, [_pad_cols(moe_w_gate[jc], dfe).astype(BF16), _pad_cols(moe_w_up[jc], dfe).astype(BF16)],
                        out_dtype=BF16, tm=tm_e, tn=512, tk=4096, epilogue=_swiglu_epilogue, group=group,
                        name="moe_up")
            ys = _matmul(u, [_pad_rows(moe_w_down[jc], dfe).astype(BF16)], out_dtype=F32, tm=tm_e, tn=1024, tk=2816,
                         group=group, name="moe_down")
            y0 = _gather_rows(ys, slot_pos[:, 0], 256)
            y1 = _gather_rows(ys, slot_pos[:, 1], 256)
            x, h = _residual_ln(x, [y0, y1], mod[l, 5], ln_g[l, 1], ln_b[l, 1], nxt, tile_group, tm_tok, alpha,
                                gates=gates)

    y_prompt = x[:tc].reshape(nbc, seq, d)
    y_sample = x[tc:].reshape(nbl, n_lat, d)
    return (y_prompt, y_sample, jnp.stack(na_k, axis=1), jnp.stack(na_v, axis=1),
            jnp.stack(mla_ckv, axis=1), jnp.stack(mla_kpe, axis=1))
```

```python
import functools
import math

import numpy as np
import jax
import jax.numpy as jnp
from jax import lax
from jax.experimental import pallas as pl
from jax.experimental.pallas import tpu as pltpu

F32 = jnp.float32
BF16 = jnp.bfloat16

LANES = 128
VMEM_LIMIT = 56 * 1024 * 1024

GRID_W = 64
N_MIXERS = 3
TOP_K = 2
ROPE_BASE = 10000.0
LN_EPS = 1e-5
RMS_EPS = 1e-6
ROPE_SWAP = 16
LOG2E = 1.4426950408889634
ATTN_TQ = 512
ATTN_CK = 1024


def _tile(dim, pref, align):
    t = (min(pref, dim) // align) * align
    while t >= align:
        if dim % t == 0:
            return t
        t -= align
    return dim


def _params(sem):
    return pltpu.CompilerParams(dimension_semantics=sem, vmem_limit_bytes=VMEM_LIMIT)


def _bf16(v):
    return v if v.dtype == BF16 else v.astype(BF16)


def _mm_kernel(*refs, nk, n_b, n_extra, grouped, epilogue):
    if grouped:
        valid_ref = refs[1]
        refs = refs[2:]
    a_ref = refs[0]
    b_refs = refs[1:1 + n_b]
    extra_refs = refs[1 + n_b:1 + n_b + n_extra]
    o_ref = refs[1 + n_b + n_extra]
    acc_refs = refs[2 + n_b + n_extra:]

    def finish(accs):
        extras = [e[...] for e in extra_refs]
        o_ref[...] = epilogue(*accs, *extras).astype(o_ref.dtype)

    def compute():
        a = _bf16(a_ref[...])
        parts = [jnp.dot(a, b[...], preferred_element_type=F32) for b in b_refs]
        if nk == 1:
            finish(parts)
            return
        k = pl.program_id(2)

        @pl.when(k == 0)
        def _():
            for acc, p in zip(acc_refs, parts):
                acc[...] = p

        @pl.when(k > 0)
        def _():
            for acc, p in zip(acc_refs, parts):
                acc[...] += p

        @pl.when(k == nk - 1)
        def _():
            finish([acc[...] for acc in acc_refs])

    if not grouped:
        compute()
        return
    valid = valid_ref[pl.program_id(0)] > 0
    pl.when(valid)(compute)

    @pl.when(jnp.logical_and(jnp.logical_not(valid), pl.program_id(2) == nk - 1))
    def _():
        o_ref[...] = jnp.zeros(o_ref.shape, o_ref.dtype)


def _identity(acc):
    return acc


def _matmul(a, bs, *, out_dtype, tm, tn, tk, epilogue=_identity, extras=(), row_tables=(), row_block=None,
            group=None, name="mm"):
    m, kdim = a.shape
    n = bs[0].shape[-1]
    tm, tn, tk = _tile(m, tm, 16), _tile(n, tn, LANES), _tile(kdim, tk, LANES)
    nk = kdim // tk
    grouped = group is not None
    if grouped:
        b_spec = pl.BlockSpec((None, tk, tn), lambda i, j, k, gid, valid: (gid[i], k, j))
    else:
        b_spec = pl.BlockSpec((tk, tn), lambda i, j, k, *_: (k, j))
    in_specs = [pl.BlockSpec((tm, tk), lambda i, j, k, *_: (i, k))]
    in_specs += [b_spec] * len(bs)
    in_specs += [pl.BlockSpec((1, tn), lambda i, j, k, *_: (0, j))] * len(extras)
    in_specs += [pl.BlockSpec((tm, tn), lambda i, j, k, *_: (row_block(i), 0))] * len(row_tables)
    extras = list(extras) + list(row_tables)
    scratch = [pltpu.VMEM((tm, tn), F32) for _ in bs] if nk > 1 else []
    kern = functools.partial(_mm_kernel, nk=nk, n_b=len(bs), n_extra=len(extras), grouped=grouped,
                             epilogue=epilogue)
    call = pl.pallas_call(
        kern,
        out_shape=jax.ShapeDtypeStruct((m, n), out_dtype),
        grid_spec=pltpu.PrefetchScalarGridSpec(
            num_scalar_prefetch=2 if grouped else 0,
            grid=(m // tm, n // tn, nk),
            in_specs=in_specs,
            out_specs=pl.BlockSpec((tm, tn), lambda i, j, k, *_: (i, j)),
            scratch_shapes=scratch),
        compiler_params=_params(("parallel", "parallel", "arbitrary")),
        name=name)
    prefetch = tuple(group) if grouped else ()
    return call(*prefetch, a, *bs, *extras)


def _swiglu_epilogue(g, u):
    return jax.nn.silu(g) * u


def _rope_scale_epilogue(scale, acc, cos, sin_up, sin_dn):
    return _rope(acc, cos, sin_up, sin_dn) * scale


def _rms_epilogue(n_norm, acc, gain):
    width = acc.shape[-1]
    if n_norm == width:
        ms = jnp.mean(jnp.square(acc), axis=-1, keepdims=True)
        return acc * lax.rsqrt(ms + RMS_EPS) * gain
    lane = lax.broadcasted_iota(jnp.int32, acc.shape, 1)
    head = lane < n_norm
    ms = jnp.sum(jnp.where(head, jnp.square(acc), 0.0), axis=-1, keepdims=True) / n_norm
    return jnp.where(head, acc * lax.rsqrt(ms + RMS_EPS) * gain, acc)


def _ada_kernel(c_ref, w_ref, b_ref, o_ref):
    s = jax.nn.silu(c_ref[...]).astype(BF16)
    o_ref[...] = jnp.dot(s, w_ref[...].astype(BF16), preferred_element_type=F32) + b_ref[...]


def _adaln(cond, w_ada, b_ada):
    depth, d, n = w_ada.shape
    g = cond.shape[0]
    tn = _tile(n, 512, LANES)
    return pl.pallas_call(
        _ada_kernel,
        out_shape=jax.ShapeDtypeStruct((depth, g, n), F32),
        grid=(depth, n // tn),
        in_specs=[pl.BlockSpec((g, d), lambda l, j: (0, 0)),
                  pl.BlockSpec((None, d, tn), lambda l, j: (l, 0, j)),
                  pl.BlockSpec((None, 1, tn), lambda l, j: (l, 0, j))],
        out_specs=pl.BlockSpec((None, g, tn), lambda l, j: (l, 0, j)),
        compiler_params=_params(("parallel", "parallel")),
        name="adaln")(cond, w_ada, b_ada.reshape(depth, 1, n))


def _modulate_kernel(gid_ref, x_ref, sh_ref, sc_ref, h_ref):
    h_ref[...] = (x_ref[...] * (1.0 + sc_ref[...]) + sh_ref[...]).astype(h_ref.dtype)


def _modulate(x, shift, scale, tile_group, tm, out_dtype):
    t, d = x.shape
    row = pl.BlockSpec((tm, d), lambda i, gid: (i, 0))
    mod = pl.BlockSpec((None, 1, d), lambda i, gid: (gid[i], 0, 0))
    return pl.pallas_call(
        _modulate_kernel,
        out_shape=jax.ShapeDtypeStruct((t, d), out_dtype),
        grid_spec=pltpu.PrefetchScalarGridSpec(
            num_scalar_prefetch=1, grid=(t // tm,), in_specs=[row, mod, mod], out_specs=row),
        compiler_params=_params(("parallel",)),
        name="modulate")(tile_group, x, shift, scale)


def _ln_kernel(gid_ref, *refs, alpha, n_y, emit_h):
    x_ref = refs[0]
    if n_y == 1:
        y = refs[1][...]
        refs = refs[2:]
    else:
        gates = refs[3][...]
        y = gates[:, 0:1] * refs[1][...] + gates[:, 1:2] * refs[2][...]
        refs = refs[4:]
    g_ref, lg_ref, lb_ref = refs[:3]
    z = alpha * x_ref[...] + g_ref[...] * y
    mu = jnp.mean(z, axis=-1, keepdims=True)
    zc = z - mu
    var = jnp.mean(jnp.square(zc), axis=-1, keepdims=True)
    xn = zc * lax.rsqrt(var + LN_EPS) * lg_ref[...] + lb_ref[...]
    if emit_h:
        sh_ref, sc_ref, xo_ref, ho_ref = refs[3:]
        xo_ref[...] = xn
        ho_ref[...] = (xn * (1.0 + sc_ref[...]) + sh_ref[...]).astype(ho_ref.dtype)
    else:
        refs[3][...] = xn


def _residual_ln(x, ys, gate, ln_g, ln_b, nxt, tile_group, tm, alpha, h_dtype=BF16, gates=None):
    t, d = x.shape
    row = pl.BlockSpec((tm, d), lambda i, gid: (i, 0))
    mod = pl.BlockSpec((None, 1, d), lambda i, gid: (gid[i], 0, 0))
    vec = pl.BlockSpec((1, d), lambda i, gid: (0, 0))
    args = [x, *ys]
    in_specs = [row] * len(args)
    if gates is not None:
        args.append(gates)
        in_specs.append(pl.BlockSpec((tm, LANES), lambda i, gid: (i, 0)))
    args += [gate, ln_g.reshape(1, d), ln_b.reshape(1, d)]
    in_specs += [mod, vec, vec]
    out_shape = [jax.ShapeDtypeStruct((t, d), F32)]
    out_specs = [row]
    if nxt is not None:
        args += list(nxt)
        in_specs += [mod, mod]
        out_shape.append(jax.ShapeDtypeStruct((t, d), h_dtype))
        out_specs.append(row)
    kern = functools.partial(_ln_kernel, alpha=alpha, n_y=len(ys), emit_h=nxt is not None)
    out = pl.pallas_call(
        kern,
        out_shape=out_shape,
        grid_spec=pltpu.PrefetchScalarGridSpec(
            num_scalar_prefetch=1, grid=(t // tm,), in_specs=in_specs, out_specs=out_specs),
        compiler_params=_params(("parallel",)),
        name="residual_ln")(tile_group, *args)
    return (out[0], out[1]) if nxt is not None else (out[0], None)


def _rope(x, cos, sin_up, sin_dn):
    w = x.shape[-1]
    return x * cos + pltpu.roll(x, w - ROPE_SWAP, 1) * sin_up + pltpu.roll(x, ROPE_SWAP, 1) * sin_dn


def _attn_kernel(q_ref, k_ref, v_ref, o_ref, *scratch, scale, sk, ck):
    if scale is None:
        qs_ref = q_ref
    else:
        qs_ref = scratch[0]
        qs_ref[...] = (q_ref[...] * (scale * LOG2E)).astype(BF16)
    acc_ref = scratch[-1]
    nt_dims = (((1,), (1,)), ((), ()))
    tn_dims = (((0,), (0,)), ((), ()))
    c0 = sk % ck if sk % ck else ck
    chunks = [(0, c0)] + [(c0 + i * ck, ck) for i in range((sk - c0) // ck)]

    def scores(start, size):
        return lax.dot_general(_bf16(k_ref[pl.ds(start, size), :]), qs_ref[...], nt_dims,
                               preferred_element_type=F32)

    def pv(p, start, size):
        return lax.dot_general(_bf16(v_ref[pl.ds(start, size), :]), p.astype(BF16), tn_dims,
                               preferred_element_type=F32)

    s_next = scores(*chunks[0])
    m = l = None
    for idx, (start, size) in enumerate(chunks):
        s = s_next
        if idx + 1 < len(chunks):
            s_next = scores(*chunks[idx + 1])
        if idx == 0:
            m = jnp.max(s, axis=0, keepdims=True)
            p = jnp.exp2(s - m)
            l = jnp.sum(p, axis=0, keepdims=True)
            acc_ref[...] = pv(p, start, size)
        else:
            m_new = jnp.maximum(m, jnp.max(s, axis=0, keepdims=True))
            a = jnp.exp2(m - m_new)
            p = jnp.exp2(s - m_new)
            l = a * l + jnp.sum(p, axis=0, keepdims=True)
            acc_ref[...] = a * acc_ref[...] + pv(p, start, size)
            m = m_new
    o_ref[...] = (acc_ref[...] / l).T.astype(o_ref.dtype)


def _attention(q_arr, k_arr, v_arr, *, nb, nh, sq, sk, dk, dv, scale, q_row0=0, k_row0=0,
               q_col0=0, k_col0=0, v_col0=0, name="attention"):
    tq = _tile(sq, ATTN_TQ, 16)
    ck = min(ATTN_CK, sk)
    assert q_row0 % tq == 0 and k_row0 % sk == 0 and (sk % ck) % LANES == 0
    nq = sq // tq
    scratch = [pltpu.VMEM((dv, tq), F32)]
    if scale is not None:
        scratch.insert(0, pltpu.VMEM((tq, dk), BF16))
    return pl.pallas_call(
        functools.partial(_attn_kernel, scale=scale, sk=sk, ck=ck),
        out_shape=jax.ShapeDtypeStruct((nb * sq, nh * dv), BF16),
        grid=(nb, nh, nq),
        in_specs=[pl.BlockSpec((tq, dk), lambda b, h, i: (q_row0 // tq + b * nq + i, q_col0 + h)),
                  pl.BlockSpec((sk, dk), lambda b, h, i: (k_row0 // sk + b, k_col0 + h)),
                  pl.BlockSpec((sk, dv), lambda b, h, i: (k_row0 // sk + b, v_col0 + h))],
        out_specs=pl.BlockSpec((tq, dv), lambda b, h, i: (b * nq + i, h)),
        scratch_shapes=scratch,
        compiler_params=_params(("parallel", "parallel", "parallel")),
        name=name)(q_arr, k_arr, v_arr)


NA_TILE_ROWS = 4
NA_KEY_ROWS = 12


def _na_tiling(rows, win_h):
    assert rows % NA_TILE_ROWS == 0 and rows >= NA_KEY_ROWS and win_h == 8
    key_start, case_of_tile, cases = [], [], []
    for t in range(rows // NA_TILE_ROWS):
        ks = int(np.clip(NA_TILE_ROWS * t - win_h // 2, 0, rows - NA_KEY_ROWS))
        idx = np.full((NA_TILE_ROWS, NA_KEY_ROWS), 2 * win_h - 1, np.int64)
        for qi in range(NA_TILE_ROWS):
            r = NA_TILE_ROWS * t + qi
            r0 = int(np.clip(r - win_h // 2, 0, rows - win_h))
            for kj in range(r0 - ks, r0 - ks + win_h):
                idx[qi, kj] = ks + kj - r + win_h - 1
        keys = [c.tobytes() for c in cases]
        if idx.tobytes() not in keys:
            cases.append(idx)
            keys.append(idx.tobytes())
        key_start.append(ks)
        case_of_tile.append(keys.index(idx.tobytes()))
    return key_start, case_of_tile, cases


def _na_kernel(q_ref, k_ref, v_ref, kc_ref, vc_ref, tab_ref, o_ref, qb_ref, kb_ref, vb_ref, *,
               key_start, case_of_tile, scale):
    qb_ref[...] = (q_ref[...] * (scale * LOG2E)).astype(BF16)
    kb_ref[...] = k_ref[...].astype(BF16)
    vb_ref[...] = v_ref[...].astype(BF16)
    kc = _bf16(kc_ref[...])
    vc = _bf16(vc_ref[...])
    nq = NA_TILE_ROWS * GRID_W
    nk = NA_KEY_ROWS * GRID_W
    nt_dims = (((1,), (1,)), ((), ()))
    tn_dims = (((0,), (0,)), ((), ()))

    def scores(t):
        q = qb_ref[pl.ds(t * nq, nq), :]
        s_loc = lax.dot_general(kb_ref[pl.ds(key_start[t] * GRID_W, nk), :], q, nt_dims,
                                preferred_element_type=F32) + tab_ref[case_of_tile[t]]
        s_ctx = lax.dot_general(kc, q, nt_dims, preferred_element_type=F32)
        return s_loc, s_ctx

    nxt = scores(0)
    for t in range(len(key_start)):
        s_loc, s_ctx = nxt
        if t + 1 < len(key_start):
            nxt = scores(t + 1)
        m = jnp.maximum(jnp.max(s_loc, axis=0, keepdims=True), jnp.max(s_ctx, axis=0, keepdims=True))
        p_loc = jnp.exp2(s_loc - m)
        p_ctx = jnp.exp2(s_ctx - m)
        l = jnp.sum(p_loc, axis=0, keepdims=True) + jnp.sum(p_ctx, axis=0, keepdims=True)
        o = (lax.dot_general(vb_ref[pl.ds(key_start[t] * GRID_W, nk), :], p_loc.astype(BF16), tn_dims,
                             preferred_element_type=F32)
             + lax.dot_general(vc, p_ctx.astype(BF16), tn_dims, preferred_element_type=F32))
        o_ref[pl.ds(t * nq, nq), :] = (o / l).T.astype(o_ref.dtype)


def _na_bias_table(rel_bias, cases):
    nh, n_dr, n_dc = rel_bias.shape
    win_w = (n_dc + 1) // 2
    col = np.arange(GRID_W)
    cs = np.clip(col - win_w // 2, 0, GRID_W - win_w)
    in_win = (col[None, :] >= cs[:, None]) & (col[None, :] < cs[:, None] + win_w)
    ext = jnp.pad(rel_bias.astype(F32) * LOG2E, ((0, 0), (0, 0), (GRID_W - win_w, GRID_W - win_w)))
    toe = jnp.stack([ext[:, :, GRID_W - 1 - q:2 * GRID_W - 1 - q] for q in range(GRID_W)], axis=-1)
    toe = jnp.where(jnp.asarray(in_win.T)[None, None], toe, -jnp.inf)
    toe = jnp.concatenate([toe, jnp.full((nh, 1, GRID_W, GRID_W), -jnp.inf, F32)], axis=1)
    tabs = []
    for idx in cases:
        rows_k = [jnp.concatenate([toe[:, idx[qi, kj]] for qi in range(idx.shape[0])], axis=-1)
                  for kj in range(idx.shape[1])]
        tabs.append(jnp.concatenate(rows_k, axis=1))
    return jnp.stack(tabs, axis=1)


def _na_latent(qkv, row0, k_ctx, v_ctx, rel_bias, *, nb, n, nh, hd):
    rows = n // GRID_W
    win_h = min((rel_bias.shape[1] + 1) // 2, rows)
    assert row0 % n == 0
    blk0 = row0 // n
    p = k_ctx.shape[0] // nb
    key_start, case_of_tile, cases = _na_tiling(rows, win_h)
    tab = _na_bias_table(rel_bias, cases)
    kern = functools.partial(_na_kernel, key_start=key_start, case_of_tile=case_of_tile, scale=hd ** -0.5)
    return pl.pallas_call(
        kern,
        out_shape=jax.ShapeDtypeStruct((nb * n, nh * hd), BF16),
        grid=(nb, nh),
        in_specs=[pl.BlockSpec((n, hd), lambda b, h: (blk0 + b, h)),
                  pl.BlockSpec((n, hd), lambda b, h: (blk0 + b, nh + h)),
                  pl.BlockSpec((n, hd), lambda b, h: (blk0 + b, 2 * nh + h)),
                  pl.BlockSpec((p, hd), lambda b, h: (b, h)),
                  pl.BlockSpec((p, hd), lambda b, h: (b, h)),
                  pl.BlockSpec((None,) + tab.shape[1:], lambda b, h: (h, 0, 0, 0))],
        out_specs=pl.BlockSpec((n, hd), lambda b, h: (b, h)),
        scratch_shapes=[pltpu.VMEM((n, hd), BF16)] * 3,
        compiler_params=_params(("parallel", "parallel")),
        name="na_latent")(qkv, qkv, qkv, k_ctx, v_ctx, tab)


def _rope_kernel(x_ref, cos_ref, su_ref, sd_ref, o_ref):
    o_ref[...] = _rope(x_ref[...], cos_ref[...], su_ref[...], sd_ref[...])


def _rope_rows(x, col_blk, row0, nb, n, tabs):
    tr = _tile(n, 1024, 8)
    assert row0 % tr == 0
    nt = n // tr
    return pl.pallas_call(
        _rope_kernel,
        out_shape=jax.ShapeDtypeStruct((nb * n, LANES), F32),
        grid=(nb, nt),
        in_specs=[pl.BlockSpec((tr, LANES), lambda b, i: (row0 // tr + b * nt + i, col_blk))]
        + [pl.BlockSpec((tr, LANES), lambda b, i: (i, 0))] * 3,
        out_specs=pl.BlockSpec((tr, LANES), lambda b, i: (b * nt + i, 0)),
        compiler_params=_params(("parallel", "parallel")),
        name="rope_kpe")(x, *tabs)


def _rope_tables(n, width, lane0, rope_dim):
    t = jnp.arange(n)
    row = (t // GRID_W).astype(F32)
    col = (t % GRID_W).astype(F32)
    half = rope_dim // 2
    nf = half // 2
    assert nf == ROPE_SWAP
    inv_freq = ROPE_BASE ** (-jnp.arange(nf, dtype=F32) / nf)
    ang_r = row[:, None] * inv_freq[None, :]
    ang_c = col[:, None] * inv_freq[None, :]
    zeros = jnp.zeros((n, nf), F32)
    cos = jnp.concatenate([jnp.cos(ang_r)] * 2 + [jnp.cos(ang_c)] * 2, axis=1)
    sin_up = jnp.concatenate([-jnp.sin(ang_r), zeros, -jnp.sin(ang_c), zeros], axis=1)
    sin_dn = jnp.concatenate([zeros, jnp.sin(ang_r), zeros, jnp.sin(ang_c)], axis=1)

    def place(tab, fill):
        left = jnp.full((n, lane0), fill, F32)
        right = jnp.full((n, width - lane0 - rope_dim), fill, F32)
        return jnp.concatenate([left, tab, right], axis=1)

    return place(cos, 1.0), place(sin_up, 0.0), place(sin_dn, 0.0)


def _dft_feat_kernel(h_ref, c_ref, s_ref, yc_ref, ys_ref):
    h = _bf16(h_ref[...])
    yc_ref[...] = jnp.dot(h, c_ref[...], preferred_element_type=F32).astype(yc_ref.dtype)
    ys_ref[...] = jnp.dot(h, s_ref[...], preferred_element_type=F32).astype(ys_ref.dtype)


def _dft_seq_kernel(c_ref, s_ref, yc_ref, ys_ref, o_ref, acc_ref, *, nk):
    k = pl.program_id(3)
    part = (jnp.dot(c_ref[...], yc_ref[...], preferred_element_type=F32)
            - jnp.dot(s_ref[...], ys_ref[...], preferred_element_type=F32))

    @pl.when(k == 0)
    def _():
        acc_ref[...] = part

    @pl.when(k > 0)
    def _():
        acc_ref[...] += part

    @pl.when(k == nk - 1)
    def _():
        o_ref[...] = acc_ref[...].astype(o_ref.dtype)


def _dft_mats(n):
    j = jnp.arange(n, dtype=jnp.int32)
    ang = ((j[:, None] * j[None, :]) % n).astype(F32) * (2.0 * math.pi / n)
    return (jnp.cos(ang) * n ** -0.5).astype(BF16), (jnp.sin(ang) * n ** -0.5).astype(BF16)


def _fourier_mix(h, segments, n_groups):
    t, d = h.shape
    dg = d // n_groups
    cd, sd = _dft_mats(dg)
    tm = _tile(t, 1024, 16)
    row = pl.BlockSpec((tm, dg), lambda i, g: (i, g))
    mat = pl.BlockSpec((dg, dg), lambda i, g: (0, 0))
    yc, ys = pl.pallas_call(
        _dft_feat_kernel,
        out_shape=[jax.ShapeDtypeStruct((t, d), BF16)] * 2,
        grid=(t // tm, n_groups),
        in_specs=[row, mat, mat],
        out_specs=[row, row],
        compiler_params=_params(("parallel", "parallel")),
        name="dft_features")(h, cd, sd)
    outs = []
    for row0, nb, s in segments:
        cs, ss = _dft_mats(s)
        tms, tks, tn = _tile(s, 1024, 16), _tile(s, 1024, LANES), _tile(d, 1024, LANES)
        assert row0 % tms == 0 and row0 % tks == 0
        ni, nk = s // tms, s // tks
        a_spec = pl.BlockSpec((tms, tks), lambda b, i, j, k: (i, k))
        y_spec = pl.BlockSpec((tks, tn), lambda b, i, j, k: (row0 // tks + b * nk + k, j))
        outs.append(pl.pallas_call(
            functools.partial(_dft_seq_kernel, nk=nk),
            out_shape=jax.ShapeDtypeStruct((nb * s, d), BF16),
            grid=(nb, ni, d // tn, nk),
            in_specs=[a_spec, a_spec, y_spec, y_spec],
            out_specs=pl.BlockSpec((tms, tn), lambda b, i, j, k: (b * ni + i, j)),
            scratch_shapes=[pltpu.VMEM((tms, tn), F32)],
            compiler_params=_params(("parallel", "parallel", "parallel", "arbitrary")),
            name="dft_tokens")(cs, ss, yc, ys))
    return jnp.concatenate(outs, axis=0)


def _router_kernel(h_ref, w_ref, b_ref, idx_ref, gate_ref):
    logits = jnp.dot(_bf16(h_ref[...]), w_ref[...], preferred_element_type=F32) + b_ref[...]
    lane = lax.broadcasted_iota(jnp.int32, logits.shape, 1)
    v1 = jnp.max(logits, axis=-1, keepdims=True)
    i1 = jnp.min(jnp.where(logits == v1, lane, LANES), axis=-1, keepdims=True)
    rest = jnp.where(lane == i1, -jnp.inf, logits)
    v2 = jnp.max(rest, axis=-1, keepdims=True)
    i2 = jnp.min(jnp.where(rest == v2, lane, LANES), axis=-1, keepdims=True)
    e2 = jnp.exp(v2 - v1)
    g1 = 1.0 / (1.0 + e2)
    g2 = e2 / (1.0 + e2)
    idx_ref[...] = jnp.where(lane == 0, i1, jnp.where(lane == 1, i2, 0))
    gate_ref[...] = jnp.where(lane == 0, g1, jnp.where(lane == 1, g2, 0.0))


def _router(h, w_router, b_router, tm):
    t, d = h.shape
    ne = w_router.shape[1]
    w = jnp.pad(w_router, ((0, 0), (0, LANES - ne))).astype(BF16)
    b = jnp.concatenate([b_router.astype(F32), jnp.full((LANES - ne,), -jnp.inf, F32)]).reshape(1, LANES)
    out = pl.BlockSpec((tm, LANES), lambda i: (i, 0))
    return pl.pallas_call(
        _router_kernel,
        out_shape=[jax.ShapeDtypeStruct((t, LANES), jnp.int32), jax.ShapeDtypeStruct((t, LANES), F32)],
        grid=(t // tm,),
        in_specs=[pl.BlockSpec((tm, d), lambda i: (i, 0)),
                  pl.BlockSpec((d, LANES), lambda i: (0, 0)),
                  pl.BlockSpec((1, LANES), lambda i: (0, 0))],
        out_specs=[out, out],
        compiler_params=_params(("parallel",)),
        name="router")(h, w, b)


def _gather_kernel(idx_ref, src_ref, o_ref, sem, *, rows):
    base = pl.program_id(0) * rows

    def copy(r, src_row):
        return pltpu.make_async_copy(src_ref.at[pl.ds(src_row, 1), :], o_ref.at[pl.ds(r, 1), :], sem)

    def issue(r, carry):
        copy(r, idx_ref[base + r]).start()
        return carry

    def drain(r, carry):
        copy(r, 0).wait()
        return carry

    lax.fori_loop(0, rows, issue, 0)
    lax.fori_loop(0, rows, drain, 0)


def _gather_rows(src, idx, rows):
    p = idx.shape[0]
    d = src.shape[1]
    rows = _tile(p, rows, 8)
    return pl.pallas_call(
        functools.partial(_gather_kernel, rows=rows),
        out_shape=jax.ShapeDtypeStruct((p, d), src.dtype),
        grid_spec=pltpu.PrefetchScalarGridSpec(
            num_scalar_prefetch=1, grid=(p // rows,),
            in_specs=[pl.BlockSpec(memory_space=pl.ANY)],
            out_specs=pl.BlockSpec((rows, d), lambda i, idx: (i, 0)),
            scratch_shapes=[pltpu.SemaphoreType.DMA]),
        compiler_params=_params(("arbitrary",)),
        name="gather_rows")(idx, src)


def _dispatch_plan(top_idx, n_experts, tm):
    t, k = top_idx.shape
    flat = top_idx.reshape(-1)
    onehot = (flat[:, None] == jnp.arange(n_experts, dtype=jnp.int32)[None, :]).astype(jnp.int32)
    rank = jnp.sum((jnp.cumsum(onehot, axis=0) - 1) * onehot, axis=1)
    counts = jnp.sum(onehot, axis=0)
    padded = ((counts + tm - 1) // tm) * tm
    ends = jnp.cumsum(padded)
    starts = ends - padded
    pos = starts[flat] + rank
    p_total = ((t * k + n_experts * (tm - 1)) // tm) * tm
    src_token = jnp.zeros((p_total,), jnp.int32).at[pos].set(jnp.arange(t * k, dtype=jnp.int32) // k)
    tile_start = jnp.arange(p_total // tm, dtype=jnp.int32) * tm
    tile_expert = jnp.minimum(jnp.sum((tile_start[:, None] >= ends[None, :]).astype(jnp.int32), axis=1),
                              n_experts - 1).astype(jnp.int32)
    tile_valid = (tile_start < ends[-1]).astype(jnp.int32)
    return src_token, pos.reshape(t, k).astype(jnp.int32), tile_expert, tile_valid


def _pad_cols(w, n):
    return jnp.pad(w, [(0, 0)] * (w.ndim - 1) + [(0, n - w.shape[-1])])


def _pad_rows(w, n):
    return jnp.pad(w, [(0, 0)] * (w.ndim - 2) + [(0, n - w.shape[-2]), (0, 0)])


def _round_up(n, m):
    return ((n + m - 1) // m) * m


def kernel(x_prompt, x_sample, cache_na_k, cache_na_v, cache_mla_ckv, cache_mla_kpe, c, c_ctx, w_ada, b_ada, ln_g, ln_b, na_w_qkv, na_w_o, na_rel_bias, mla_w_dq, mla_q_norm, mla_w_uq, mla_w_dkv, mla_kv_norm, mla_w_uk, mla_w_uv, mla_w_o, fnet_w_o, ffn_w_gate, ffn_w_up, ffn_w_down, moe_w_router, moe_b_router, moe_w_gate, moe_w_up, moe_w_down):
    nbc, seq, d = x_prompt.shape
    nbl, n_lat, _ = x_sample.shape
    depth = w_ada.shape[0]
    alpha = (2 * depth) ** 0.25
    tc, tl = nbc * seq, nbl * n_lat
    t = tc + tl
    past = cache_na_k.shape[2]
    na_heads, na_hd = cache_na_k.shape[3], cache_na_k.shape[4]
    kv_lora, qk_rope = cache_mla_ckv.shape[-1], cache_mla_kpe.shape[-1]
    mla_heads, qk_dim = mla_w_uq.shape[2], mla_w_uq.shape[3]
    qk_nope = qk_dim - qk_rope
    v_dim = mla_w_uv.shape[-1]
    n_experts = moe_w_router.shape[-1]
    fnet_groups = 8
    assert qk_nope == LANES and v_dim == LANES and na_hd == LANES and qk_rope <= LANES // 2
    hq = 2 * LANES

    tm_tok = _tile(math.gcd(tc, n_lat), 128, 8)
    tile_group = jnp.asarray([0 if i * tm_tok < tc else 1 + (i * tm_tok - tc) // n_lat
                              for i in range(t // tm_tok)], jnp.int32)

    x = jnp.concatenate([x_prompt.reshape(tc, d), x_sample.reshape(tl, d)], axis=0)
    n_cond = 1 + nbl
    cond = jnp.concatenate([c_ctx[None, :], c, jnp.zeros((_round_up(n_cond, 8) - n_cond, d), F32)], axis=0)
    mod = _adaln(cond, w_ada, b_ada)
    mod = mod.reshape(depth, cond.shape[0], 6, 1, d).transpose(0, 2, 1, 3, 4)

    h = _modulate(x, mod[0, 0], mod[0, 1], tile_group, tm_tok, BF16)
    na_k, na_v, mla_ckv, mla_kpe = [], [], [], []

    for l in range(depth):
        kind, j = l % N_MIXERS, l // N_MIXERS
        if kind == 0:
            qkv = _matmul(h, [na_w_qkv[j].astype(BF16)], out_dtype=F32, tm=1024, tn=1024, tk=4096, name="na_qkv")
            na_k.append(qkv[:tc, d:2 * d].reshape(nbc, seq, na_heads, na_hd))
            na_v.append(qkv[:tc, 2 * d:].reshape(nbc, seq, na_heads, na_hd))
            o_ctx = _attention(qkv, qkv, qkv, nb=nbc, nh=na_heads, sq=seq, sk=seq, dk=na_hd, dv=na_hd,
                               scale=na_hd ** -0.5, k_col0=na_heads, v_col0=2 * na_heads, name="na_context")
            o_lat = _na_latent(qkv, tc, cache_na_k[:, j].reshape(nbl * past, d),
                               cache_na_v[:, j].reshape(nbl * past, d), na_rel_bias[j],
                               nb=nbl, n=n_lat, nh=na_heads, hd=na_hd)
            w_o = na_w_o[j]
        elif kind == 1:
            scale = qk_dim ** -0.5
            cq = _matmul(h, [mla_w_dq[j].astype(BF16)], out_dtype=BF16, tm=1024, tn=mla_w_dq.shape[-1], tk=4096,
                         epilogue=functools.partial(_rms_epilogue, mla_w_dq.shape[-1]),
                         extras=[mla_q_norm[j].reshape(1, -1)], name="mla_dq")
            w_uq = _pad_cols(mla_w_uq[j], hq).reshape(-1, mla_heads * hq).astype(BF16)
            tm_q = _tile(math.gcd(tc, n_lat), 1024, 16)
            tn_q = _tile(mla_heads * hq, 1024, hq)
            ident = (jnp.ones((tm_q, hq), F32), jnp.zeros((tm_q, hq), F32), jnp.zeros((tm_q, hq), F32))
            q_tabs = [jnp.tile(jnp.concatenate([i_tab, r_tab], axis=0), (1, tn_q // hq))
                      for i_tab, r_tab in zip(ident, _rope_tables(n_lat, hq, qk_nope, qk_rope))]
            q = _matmul(cq, [w_uq], out_dtype=BF16, tm=tm_q, tn=tn_q, tk=4096,
                        epilogue=functools.partial(_rope_scale_epilogue, scale * LOG2E), row_tables=q_tabs,
                        row_block=lambda i: jnp.where(i < tc // tm_q, 0, 1 + (i - tc // tm_q) % (n_lat // tm_q)),
                        name="mla_uq")
            kvw = _round_up(kv_lora + qk_rope, LANES)
            kv = _matmul(h, [_pad_cols(mla_w_dkv[j], kvw).astype(BF16)], out_dtype=F32, tm=1024, tn=kvw, tk=4096,
                         epilogue=functools.partial(_rms_epilogue, kv_lora),
                         extras=[_pad_cols(mla_kv_norm[j].reshape(1, -1), kvw)], name="mla_dkv")
            mla_ckv.append(kv[:tc, :kv_lora].reshape(nbc, seq, kv_lora))
            mla_kpe.append(kv[:tc, kv_lora:kv_lora + qk_rope].reshape(nbc, seq, qk_rope))
            kpe_lat = _rope_rows(kv, kv_lora // LANES, tc, nbl, n_lat, _rope_tables(n_lat, LANES, 0, qk_rope))
            ckv_lat = jnp.concatenate([kv[tc:, :kv_lora], kpe_lat], axis=1).reshape(nbl, n_lat, kvw)
            ckv_past = _pad_cols(jnp.concatenate([cache_mla_ckv[:, j], cache_mla_kpe[:, j]], axis=-1), kvw)
            ckv_all = jnp.concatenate([ckv_lat, ckv_past], axis=1).reshape(nbl * (n_lat + past), kvw).astype(BF16)
            ckv_ctx = kv[:tc].astype(BF16)
            eye = jnp.pad(jnp.eye(qk_rope, dtype=F32), ((0, kvw - kv_lora - qk_rope), (0, 0)))
            w_k = jnp.concatenate([
                jnp.concatenate([mla_w_uk[j], jnp.zeros((kvw - kv_lora, mla_heads, qk_nope), F32)], axis=0),
                jnp.broadcast_to(jnp.concatenate([jnp.zeros((kv_lora, qk_rope), F32), eye], axis=0)[:, None, :],
                                 (kvw, mla_heads, qk_rope)),
                jnp.zeros((kvw, mla_heads, hq - qk_dim), F32)], axis=-1).reshape(kvw, mla_heads * hq).astype(BF16)
            w_v = _pad_rows(mla_w_uv[j].reshape(kv_lora, mla_heads * v_dim), kvw).astype(BF16)
            k_ctx = _matmul(ckv_ctx, [w_k], out_dtype=BF16, tm=1024, tn=1024, tk=kvw, name="mla_k_ctx")
            v_ctx = _matmul(ckv_ctx, [w_v], out_dtype=BF16, tm=1024, tn=1024, tk=kvw, name="mla_v_ctx")
            k_lat = _matmul(ckv_all, [w_k], out_dtype=BF16, tm=1024, tn=1024, tk=kvw, name="mla_k_lat")
            v_lat = _matmul(ckv_all, [w_v], out_dtype=BF16, tm=1024, tn=1024, tk=kvw, name="mla_v_lat")
            o_ctx = _attention(q, k_ctx, v_ctx, nb=nbc, nh=mla_heads, sq=seq, sk=seq, dk=hq, dv=v_dim,
                               scale=None, name="mla_context")
            o_lat = _attention(q, k_lat, v_lat, nb=nbl, nh=mla_heads, sq=n_lat, sk=n_lat + past, dk=hq, dv=v_dim,
                               scale=None, q_row0=tc, name="mla_latent")
            w_o = mla_w_o[j]
        else:
            o_ctx = None
            o_lat = _fourier_mix(h, [(0, nbc, seq), (tc, nbl, n_lat)], fnet_groups)
            w_o = fnet_w_o[j]
        o = o_lat if o_ctx is None else jnp.concatenate([o_ctx, o_lat], axis=0)
        y = _matmul(o, [w_o.astype(BF16)], out_dtype=F32, tm=1024, tn=1024, tk=4096, name="mixer_out")

        jc = l // 2
        moe = l % 2 == 1
        x, h = _residual_ln(x, [y], mod[l, 2], ln_g[l, 0], ln_b[l, 0], (mod[l, 3], mod[l, 4]), tile_group, tm_tok,
                            alpha, h_dtype=F32 if moe else BF16)
        nxt = (mod[l + 1, 0], mod[l + 1, 1]) if l + 1 < depth else None
        if not moe:
            dff = _round_up(ffn_w_gate.shape[-1], 1024)
            u = _matmul(h, [_pad_cols(ffn_w_gate[jc], dff).astype(BF16), _pad_cols(ffn_w_up[jc], dff).astype(BF16)],
                        out_dtype=BF16, tm=1024, tn=512, tk=4096, epilogue=_swiglu_epilogue, name="ffn_up")
            y = _matmul(u, [_pad_rows(ffn_w_down[jc], dff).astype(BF16)], out_dtype=F32, tm=1024, tn=1024, tk=2816,
                        name="ffn_down")
            x, h = _residual_ln(x, [y], mod[l, 5], ln_g[l, 1], ln_b[l, 1], nxt, tile_group, tm_tok, alpha)
        else:
            tm_e = 512
            dfe = _round_up(moe_w_gate.shape[-1], 512)
            idx, gates = _router(h, moe_w_router[jc], moe_b_router[jc], tm_tok)
            src_token, slot_pos, tile_expert, tile_valid = _dispatch_plan(idx[:, :TOP_K], n_experts, tm_e)
            hs = _gather_rows(h, src_token, 256)
            group = (tile_expert, tile_valid)
            u = _matmul(hs, [_pad_cols(moe_w_gate[jc], dfe).astype(BF16), _pad_cols(moe_w_up[jc], dfe).astype(BF16)],
                        out_dtype=BF16, tm=tm_e, tn=512, tk=4096, epilogue=_swiglu_epilogue, group=group,
                        name="moe_up")
            ys = _matmul(u, [_pad_rows(moe_w_down[jc], dfe).astype(BF16)], out_dtype=F32, tm=tm_e, tn=1024, tk=2816,
                         group=group, name="moe_down")
            y0 = _gather_rows(ys, slot_pos[:, 0], 256)
            y1 = _gather_rows(ys, slot_pos[:, 1], 256)
            x, h = _residual_ln(x, [y0, y1], mod[l, 5], ln_g[l, 1], ln_b[l, 1], nxt, tile_group, tm_tok, alpha,
                                gates=gates)

    y_prompt = x[:tc].reshape(nbc, seq, d)
    y_sample = x[tc:].reshape(nbl, n_lat, d)
    return (y_prompt, y_sample, jnp.stack(na_k, axis=1), jnp.stack(na_v, axis=1),
            jnp.stack(mla_ckv, axis=1), jnp.stack(mla_kpe, axis=1))
```

```python
import functools
import math

import numpy as np
import jax
import jax.numpy as jnp
from jax import lax
from jax.experimental import pallas as pl
from jax.experimental.pallas import tpu as pltpu

F32 = jnp.float32
BF16 = jnp.bfloat16

LANES = 128
VMEM_LIMIT = 56 * 1024 * 1024

GRID_W = 64
N_MIXERS = 3
TOP_K = 2
ROPE_BASE = 10000.0
LN_EPS = 1e-5
RMS_EPS = 1e-6
ROPE_SWAP = 16
LOG2E = 1.4426950408889634
ATTN_TQ = 512
ATTN_CK = 1024
CTX_HEADS = 4


def _tile(dim, pref, align):
    t = (min(pref, dim) // align) * align
    while t >= align:
        if dim % t == 0:
            return t
        t -= align
    return dim


def _params(sem):
    return pltpu.CompilerParams(dimension_semantics=sem, vmem_limit_bytes=VMEM_LIMIT)


def _bf16(v):
    return v if v.dtype == BF16 else v.astype(BF16)


def _mm_kernel(*refs, nk, n_b, n_extra, grouped, epilogue):
    if grouped:
        valid_ref = refs[1]
        refs = refs[2:]
    a_ref = refs[0]
    b_refs = refs[1:1 + n_b]
    extra_refs = refs[1 + n_b:1 + n_b + n_extra]
    o_ref = refs[1 + n_b + n_extra]
    acc_refs = refs[2 + n_b + n_extra:]

    def finish(accs):
        extras = [e[...] for e in extra_refs]
        o_ref[...] = epilogue(*accs, *extras).astype(o_ref.dtype)

    def compute():
        a = _bf16(a_ref[...])
        parts = [jnp.dot(a, b[...], preferred_element_type=F32) for b in b_refs]
        if nk == 1:
            finish(parts)
            return
        k = pl.program_id(2)

        @pl.when(k == 0)
        def _():
            for acc, p in zip(acc_refs, parts):
                acc[...] = p

        @pl.when(k > 0)
        def _():
            for acc, p in zip(acc_refs, parts):
                acc[...] += p

        @pl.when(k == nk - 1)
        def _():
            finish([acc[...] for acc in acc_refs])

    if not grouped:
        compute()
        return
    valid = valid_ref[pl.program_id(0)] > 0
    pl.when(valid)(compute)

    @pl.when(jnp.logical_and(jnp.logical_not(valid), pl.program_id(2) == nk - 1))
    def _():
        o_ref[...] = jnp.zeros(o_ref.shape, o_ref.dtype)


def _identity(acc):
    return acc


def _matmul(a, bs, *, out_dtype, tm, tn, tk, epilogue=_identity, extras=(), row_tables=(), row_block=None,
            group=None, name="mm"):
    m, kdim = a.shape
    n = bs[0].shape[-1]
    tm, tn, tk = _tile(m, tm, 16), _tile(n, tn, LANES), _tile(kdim, tk, LANES)
    nk = kdim // tk
    grouped = group is not None
    if grouped:
        b_spec = pl.BlockSpec((None, tk, tn), lambda i, j, k, gid, valid: (gid[i], k, j))
    else:
        b_spec = pl.BlockSpec((tk, tn), lambda i, j, k, *_: (k, j))
    in_specs = [pl.BlockSpec((tm, tk), lambda i, j, k, *_: (i, k))]
    in_specs += [b_spec] * len(bs)
    in_specs += [pl.BlockSpec((1, tn), lambda i, j, k, *_: (0, j))] * len(extras)
    in_specs += [pl.BlockSpec((tm, tn), lambda i, j, k, *_: (row_block(i), 0))] * len(row_tables)
    extras = list(extras) + list(row_tables)
    scratch = [pltpu.VMEM((tm, tn), F32) for _ in bs] if nk > 1 else []
    kern = functools.partial(_mm_kernel, nk=nk, n_b=len(bs), n_extra=len(extras), grouped=grouped,
                             epilogue=epilogue)
    call = pl.pallas_call(
        kern,
        out_shape=jax.ShapeDtypeStruct((m, n), out_dtype),
        grid_spec=pltpu.PrefetchScalarGridSpec(
            num_scalar_prefetch=2 if grouped else 0,
            grid=(m // tm, n // tn, nk),
            in_specs=in_specs,
            out_specs=pl.BlockSpec((tm, tn), lambda i, j, k, *_: (i, j)),
            scratch_shapes=scratch),
        compiler_params=_params(("parallel", "parallel", "arbitrary")),
        name=name)
    prefetch = tuple(group) if grouped else ()
    return call(*prefetch, a, *bs, *extras)


def _cast_kernel(w_ref, o_ref, *, rows, cols):
    tr, tc = o_ref.shape
    r = lax.broadcasted_iota(jnp.int32, (tr, tc), 0) + pl.program_id(1) * tr
    c = lax.broadcasted_iota(jnp.int32, (tr, tc), 1) + pl.program_id(2) * tc
    o_ref[...] = jnp.where(jnp.logical_and(r < rows, c < cols), w_ref[...], 0.0).astype(o_ref.dtype)


def _cast_weights(w, first, count, rows_out=None, cols_out=None):
    _, rows, cols = w.shape
    rows_out, cols_out = rows_out or rows, cols_out or cols
    tc = _tile(cols_out, 1024, LANES)
    tr = _tile(rows_out, (1 << 20) // tc, 16)
    out = pl.pallas_call(
        functools.partial(_cast_kernel, rows=rows, cols=cols),
        out_shape=jax.ShapeDtypeStruct((count, rows_out, cols_out), BF16),
        grid=(count, rows_out // tr, cols_out // tc),
        in_specs=[pl.BlockSpec((None, tr, tc), lambda g, i, j: (first + g, i, j))],
        out_specs=pl.BlockSpec((None, tr, tc), lambda g, i, j: (g, i, j)),
        compiler_params=_params(("parallel", "parallel", "parallel")),
        name="cast_weights")(w)
    return out.reshape(rows_out, cols_out) if count == 1 else out


def _swiglu_epilogue(g, u):
    return jax.nn.silu(g) * u


def _rope_scale_epilogue(scale, acc, cos, sin_up, sin_dn):
    return _rope(acc, cos, sin_up, sin_dn) * scale


def _rms_epilogue(n_norm, acc, gain):
    width = acc.shape[-1]
    if n_norm == width:
        ms = jnp.mean(jnp.square(acc), axis=-1, keepdims=True)
        return acc * lax.rsqrt(ms + RMS_EPS) * gain
    lane = lax.broadcasted_iota(jnp.int32, acc.shape, 1)
    head = lane < n_norm
    ms = jnp.sum(jnp.where(head, jnp.square(acc), 0.0), axis=-1, keepdims=True) / n_norm
    return jnp.where(head, acc * lax.rsqrt(ms + RMS_EPS) * gain, acc)


def _ada_kernel(c_ref, w_ref, b_ref, o_ref):
    s = jax.nn.silu(c_ref[...]).astype(BF16)
    o_ref[...] = jnp.dot(s, w_ref[...].astype(BF16), preferred_element_type=F32) + b_ref[...]


def _adaln(cond, w_ada, b_ada):
    depth, d, n = w_ada.shape
    g = cond.shape[0]
    tn = _tile(n, 512, LANES)
    return pl.pallas_call(
        _ada_kernel,
        out_shape=jax.ShapeDtypeStruct((depth, g, n), F32),
        grid=(depth, n // tn),
        in_specs=[pl.BlockSpec((g, d), lambda l, j: (0, 0)),
                  pl.BlockSpec((None, d, tn), lambda l, j: (l, 0, j)),
                  pl.BlockSpec((None, 1, tn), lambda l, j: (l, 0, j))],
        out_specs=pl.BlockSpec((None, g, tn), lambda l, j: (l, 0, j)),
        compiler_params=_params(("parallel", "parallel")),
        name="adaln")(cond, w_ada, b_ada.reshape(depth, 1, n))


def _modulate_kernel(gid_ref, x_ref, sh_ref, sc_ref, h_ref):
    h_ref[...] = (x_ref[...] * (1.0 + sc_ref[...]) + sh_ref[...]).astype(h_ref.dtype)


def _modulate(x, shift, scale, tile_group, tm, out_dtype):
    t, d = x.shape
    row = pl.BlockSpec((tm, d), lambda i, gid: (i, 0))
    mod = pl.BlockSpec((None, 1, d), lambda i, gid: (gid[i], 0, 0))
    return pl.pallas_call(
        _modulate_kernel,
        out_shape=jax.ShapeDtypeStruct((t, d), out_dtype),
        grid_spec=pltpu.PrefetchScalarGridSpec(
            num_scalar_prefetch=1, grid=(t // tm,), in_specs=[row, mod, mod], out_specs=row),
        compiler_params=_params(("parallel",)),
        name="modulate")(tile_group, x, shift, scale)


def _ln_kernel(gid_ref, *refs, alpha, n_y, emit_h):
    x_ref = refs[0]
    if n_y == 1:
        y = refs[1][...]
        refs = refs[2:]
    else:
        gates = refs[3][...]
        y = gates[:, 0:1] * refs[1][...] + gates[:, 1:2] * refs[2][...]
        refs = refs[4:]
    g_ref, lg_ref, lb_ref = refs[:3]
    z = alpha * x_ref[...] + g_ref[...] * y
    mu = jnp.mean(z, axis=-1, keepdims=True)
    zc = z - mu
    var = jnp.mean(jnp.square(zc), axis=-1, keepdims=True)
    xn = zc * lax.rsqrt(var + LN_EPS) * lg_ref[...] + lb_ref[...]
    if emit_h:
        sh_ref, sc_ref, xo_ref, ho_ref = refs[3:]
        xo_ref[...] = xn
        ho_ref[...] = (xn * (1.0 + sc_ref[...]) + sh_ref[...]).astype(ho_ref.dtype)
    else:
        refs[3][...] = xn


def _residual_ln(x, ys, gate, ln_g, ln_b, nxt, tile_group, tm, alpha, h_dtype=BF16, gates=None):
    t, d = x.shape
    row = pl.BlockSpec((tm, d), lambda i, gid: (i, 0))
    mod = pl.BlockSpec((None, 1, d), lambda i, gid: (gid[i], 0, 0))
    vec = pl.BlockSpec((1, d), lambda i, gid: (0, 0))
    args = [x, *ys]
    in_specs = [row] * len(args)
    if gates is not None:
        args.append(gates)
        in_specs.append(pl.BlockSpec((tm, LANES), lambda i, gid: (i, 0)))
    args += [gate, ln_g.reshape(1, d), ln_b.reshape(1, d)]
    in_specs += [mod, vec, vec]
    out_shape = [jax.ShapeDtypeStruct((t, d), F32)]
    out_specs = [row]
    if nxt is not None:
        args += list(nxt)
        in_specs += [mod, mod]
        out_shape.append(jax.ShapeDtypeStruct((t, d), h_dtype))
        out_specs.append(row)
    kern = functools.partial(_ln_kernel, alpha=alpha, n_y=len(ys), emit_h=nxt is not None)
    out = pl.pallas_call(
        kern,
        out_shape=out_shape,
        grid_spec=pltpu.PrefetchScalarGridSpec(
            num_scalar_prefetch=1, grid=(t // tm,), in_specs=in_specs, out_specs=out_specs),
        compiler_params=_params(("parallel",)),
        name="residual_ln")(tile_group, *args)
    return (out[0], out[1]) if nxt is not None else (out[0], None)


def _rope(x, cos, sin_up, sin_dn):
    w = x.shape[-1]
    return x * cos + pltpu.roll(x, w - ROPE_SWAP, 1) * sin_up + pltpu.roll(x, ROPE_SWAP, 1) * sin_dn


def _attn_kernel(q_ref, k_ref, v_ref, o_ref, *scratch, scale, sk, ck, heads, dk, dv):
    acc_ref = scratch[-1]
    nt_dims = (((1,), (1,)), ((), ()))
    tn_dims = (((0,), (0,)), ((), ()))
    c0 = sk % ck if sk % ck else ck
    chunks = [(0, c0)] + [(c0 + i * ck, ck) for i in range((sk - c0) // ck)]

    def one_head(kcols, vcols):
        if scale is None:
            def q_tile():
                return q_ref[:, kcols]
        else:
            qs_ref = scratch[0]
            qs_ref[...] = (q_ref[:, kcols] * (scale * LOG2E)).astype(BF16)

            def q_tile():
                return qs_ref[...]

        def scores(start, size):
            return lax.dot_general(_bf16(k_ref[pl.ds(start, size), kcols]), q_tile(), nt_dims,
                                   preferred_element_type=F32)

        def pv(p, start, size):
            return lax.dot_general(_bf16(v_ref[pl.ds(start, size), vcols]), p.astype(BF16), tn_dims,
                                   preferred_element_type=F32)

        s_next = scores(*chunks[0])
        m = l = None
        for idx, (start, size) in enumerate(chunks):
            s = s_next
            if idx + 1 < len(chunks):
                s_next = scores(*chunks[idx + 1])
            if idx == 0:
                m = jnp.max(s, axis=0, keepdims=True)
                p = jnp.exp2(s - m)
                l = jnp.sum(p, axis=0, keepdims=True)
                acc_ref[...] = pv(p, start, size)
            else:
                m_new = jnp.maximum(m, jnp.max(s, axis=0, keepdims=True))
                a = jnp.exp2(m - m_new)
                p = jnp.exp2(s - m_new)
                l = a * l + jnp.sum(p, axis=0, keepdims=True)
                acc_ref[...] = a * acc_ref[...] + pv(p, start, size)
                m = m_new
        o_ref[:, vcols] = (acc_ref[...] / l).T.astype(o_ref.dtype)

    for hh in range(heads):
        one_head(pl.ds(hh * dk, dk), pl.ds(hh * dv, dv))


def _attention(q_arr, k_arr, v_arr, *, nb, nh, sq, sk, dk, dv, scale, q_row0=0, k_row0=0,
               q_col0=0, k_col0=0, v_col0=0, heads=1, name="attention"):
    tq = _tile(sq, ATTN_TQ, 16)
    ck = min(ATTN_CK, sk)
    assert q_row0 % tq == 0 and k_row0 % sk == 0 and (sk % ck) % LANES == 0
    assert nh % heads == 0 and q_col0 % heads == 0 and k_col0 % heads == 0 and v_col0 % heads == 0
    nq = sq // tq
    scratch = [pltpu.VMEM((dv, tq), F32)]
    if scale is not None:
        scratch.insert(0, pltpu.VMEM((tq, dk), BF16))
    return pl.pallas_call(
        functools.partial(_attn_kernel, scale=scale, sk=sk, ck=ck, heads=heads, dk=dk, dv=dv),
        out_shape=jax.ShapeDtypeStruct((nb * sq, nh * dv), BF16),
        grid=(nb, nh // heads, nq),
        in_specs=[pl.BlockSpec((tq, heads * dk), lambda b, h, i: (q_row0 // tq + b * nq + i, q_col0 // heads + h)),
                  pl.BlockSpec((sk, heads * dk), lambda b, h, i: (k_row0 // sk + b, k_col0 // heads + h)),
                  pl.BlockSpec((sk, heads * dv), lambda b, h, i: (k_row0 // sk + b, v_col0 // heads + h))],
        out_specs=pl.BlockSpec((tq, heads * dv), lambda b, h, i: (b * nq + i, h)),
        scratch_shapes=scratch,
        compiler_params=_params(("parallel", "parallel", "parallel")),
        name=name)(q_arr, k_arr, v_arr)


NA_TILE_ROWS = 4
NA_KEY_ROWS = 12


def _na_tiling(rows, win_h):
    assert rows % NA_TILE_ROWS == 0 and rows >= NA_KEY_ROWS and win_h == 8
    key_start, case_of_tile, cases = [], [], []
    for t in range(rows // NA_TILE_ROWS):
        ks = int(np.clip(NA_TILE_ROWS * t - win_h // 2, 0, rows - NA_KEY_ROWS))
        idx = np.full((NA_TILE_ROWS, NA_KEY_ROWS), 2 * win_h - 1, np.int64)
        for qi in range(NA_TILE_ROWS):
            r = NA_TILE_ROWS * t + qi
            r0 = int(np.clip(r - win_h // 2, 0, rows - win_h))
            for kj in range(r0 - ks, r0 - ks + win_h):
                idx[qi, kj] = ks + kj - r + win_h - 1
        keys = [c.tobytes() for c in cases]
        if idx.tobytes() not in keys:
            cases.append(idx)
            keys.append(idx.tobytes())
        key_start.append(ks)
        case_of_tile.append(keys.index(idx.tobytes()))
    return key_start, case_of_tile, cases


def _na_kernel(q_ref, k_ref, v_ref, kc_ref, vc_ref, tab_ref, o_ref, qb_ref, kb_ref, vb_ref, *,
               key_start, case_of_tile, scale):
    qb_ref[...] = (q_ref[...] * (scale * LOG2E)).astype(BF16)
    kb_ref[...] = k_ref[...].astype(BF16)
    vb_ref[...] = v_ref[...].astype(BF16)
    kc = _bf16(kc_ref[...])
    vc = _bf16(vc_ref[...])
    nq = NA_TILE_ROWS * GRID_W
    nk = NA_KEY_ROWS * GRID_W
    nt_dims = (((1,), (1,)), ((), ()))
    tn_dims = (((0,), (0,)), ((), ()))

    def scores(t):
        q = qb_ref[pl.ds(t * nq, nq), :]
        s_loc = lax.dot_general(kb_ref[pl.ds(key_start[t] * GRID_W, nk), :], q, nt_dims,
                                preferred_element_type=F32) + tab_ref[case_of_tile[t]]
        s_ctx = lax.dot_general(kc, q, nt_dims, preferred_element_type=F32)
        return s_loc, s_ctx

    nxt = scores(0)
    for t in range(len(key_start)):
        s_loc, s_ctx = nxt
        if t + 1 < len(key_start):
            nxt = scores(t + 1)
        m = jnp.maximum(jnp.max(s_loc, axis=0, keepdims=True), jnp.max(s_ctx, axis=0, keepdims=True))
        p_loc = jnp.exp2(s_loc - m)
        p_ctx = jnp.exp2(s_ctx - m)
        l = jnp.sum(p_loc, axis=0, keepdims=True) + jnp.sum(p_ctx, axis=0, keepdims=True)
        o = (lax.dot_general(vb_ref[pl.ds(key_start[t] * GRID_W, nk), :], p_loc.astype(BF16), tn_dims,
                             preferred_element_type=F32)
             + lax.dot_general(vc, p_ctx.astype(BF16), tn_dims, preferred_element_type=F32))
        o_ref[pl.ds(t * nq, nq), :] = (o / l).T.astype(o_ref.dtype)


def _na_bias_table(rel_bias, cases):
    nh, n_dr, n_dc = rel_bias.shape
    win_w = (n_dc + 1) // 2
    col = np.arange(GRID_W)
    cs = np.clip(col - win_w // 2, 0, GRID_W - win_w)
    in_win = (col[None, :] >= cs[:, None]) & (col[None, :] < cs[:, None] + win_w)
    ext = jnp.pad(rel_bias.astype(F32) * LOG2E, ((0, 0), (0, 0), (GRID_W - win_w, GRID_W - win_w)))
    toe = jnp.stack([ext[:, :, GRID_W - 1 - q:2 * GRID_W - 1 - q] for q in range(GRID_W)], axis=-1)
    toe = jnp.where(jnp.asarray(in_win.T)[None, None], toe, -jnp.inf)
    toe = jnp.concatenate([toe, jnp.full((nh, 1, GRID_W, GRID_W), -jnp.inf, F32)], axis=1)
    tabs = []
    for idx in cases:
        rows_k = [jnp.concatenate([toe[:, idx[qi, kj]] for qi in range(idx.shape[0])], axis=-1)
                  for kj in range(idx.shape[1])]
        tabs.append(jnp.concatenate(rows_k, axis=1))
    return jnp.stack(tabs, axis=1)


def _na_latent(qkv, row0, k_ctx, v_ctx, rel_bias, *, nb, n, nh, hd):
    rows = n // GRID_W
    win_h = min((rel_bias.shape[1] + 1) // 2, rows)
    assert row0 % n == 0
    blk0 = row0 // n
    p = k_ctx.shape[0] // nb
    key_start, case_of_tile, cases = _na_tiling(rows, win_h)
    tab = _na_bias_table(rel_bias, cases)
    kern = functools.partial(_na_kernel, key_start=key_start, case_of_tile=case_of_tile, scale=hd ** -0.5)
    return pl.pallas_call(
        kern,
        out_shape=jax.ShapeDtypeStruct((nb * n, nh * hd), BF16),
        grid=(nb, nh),
        in_specs=[pl.BlockSpec((n, hd), lambda b, h: (blk0 + b, h)),
                  pl.BlockSpec((n, hd), lambda b, h: (blk0 + b, nh + h)),
                  pl.BlockSpec((n, hd), lambda b, h: (blk0 + b, 2 * nh + h)),
                  pl.BlockSpec((p, hd), lambda b, h: (b, h)),
                  pl.BlockSpec((p, hd), lambda b, h: (b, h)),
                  pl.BlockSpec((None,) + tab.shape[1:], lambda b, h: (h, 0, 0, 0))],
        out_specs=pl.BlockSpec((n, hd), lambda b, h: (b, h)),
        scratch_shapes=[pltpu.VMEM((n, hd), BF16)] * 3,
        compiler_params=_params(("parallel", "parallel")),
        name="na_latent")(qkv, qkv, qkv, k_ctx, v_ctx, tab)


def _rope_kernel(x_ref, cos_ref, su_ref, sd_ref, o_ref):
    o_ref[...] = _rope(x_ref[...], cos_ref[...], su_ref[...], sd_ref[...])


def _rope_rows(x, col_blk, row0, nb, n, tabs):
    tr = _tile(n, 1024, 8)
    assert row0 % tr == 0
    nt = n // tr
    return pl.pallas_call(
        _rope_kernel,
        out_shape=jax.ShapeDtypeStruct((nb * n, LANES), F32),
        grid=(nb, nt),
        in_specs=[pl.BlockSpec((tr, LANES), lambda b, i: (row0 // tr + b * nt + i, col_blk))]
        + [pl.BlockSpec((tr, LANES), lambda b, i: (i, 0))] * 3,
        out_specs=pl.BlockSpec((tr, LANES), lambda b, i: (b * nt + i, 0)),
        compiler_params=_params(("parallel", "parallel")),
        name="rope_kpe")(x, *tabs)


def _rope_tables(n, width, lane0, rope_dim):
    t = jnp.arange(n)
    row = (t // GRID_W).astype(F32)
    col = (t % GRID_W).astype(F32)
    half = rope_dim // 2
    nf = half // 2
    assert nf == ROPE_SWAP
    inv_freq = ROPE_BASE ** (-jnp.arange(nf, dtype=F32) / nf)
    ang_r = row[:, None] * inv_freq[None, :]
    ang_c = col[:, None] * inv_freq[None, :]
    zeros = jnp.zeros((n, nf), F32)
    cos = jnp.concatenate([jnp.cos(ang_r)] * 2 + [jnp.cos(ang_c)] * 2, axis=1)
    sin_up = jnp.concatenate([-jnp.sin(ang_r), zeros, -jnp.sin(ang_c), zeros], axis=1)
    sin_dn = jnp.concatenate([zeros, jnp.sin(ang_r), zeros, jnp.sin(ang_c)], axis=1)

    def place(tab, fill):
        left = jnp.full((n, lane0), fill, F32)
        right = jnp.full((n, width - lane0 - rope_dim), fill, F32)
        return jnp.concatenate([left, tab, right], axis=1)

    return place(cos, 1.0), place(sin_up, 0.0), place(sin_dn, 0.0)


def _dft_feat_kernel(h_ref, c_ref, s_ref, yc_ref, ys_ref):
    h = _bf16(h_ref[...])
    yc_ref[...] = jnp.dot(h, c_ref[...], preferred_element_type=F32).astype(yc_ref.dtype)
    ys_ref[...] = jnp.dot(h, s_ref[...], preferred_element_type=F32).astype(ys_ref.dtype)


def _dft_seq_kernel(c_ref, s_ref, yc_ref, ys_ref, o_ref, acc_ref, *, nk):
    k = pl.program_id(3)
    part = (jnp.dot(c_ref[...], yc_ref[...], preferred_element_type=F32)
            - jnp.dot(s_ref[...], ys_ref[...], preferred_element_type=F32))

    @pl.when(k == 0)
    def _():
        acc_ref[...] = part

    @pl.when(k > 0)
    def _():
        acc_ref[...] += part

    @pl.when(k == nk - 1)
    def _():
        o_ref[...] = acc_ref[...].astype(o_ref.dtype)


def _dft_mats(n):
    j = jnp.arange(n, dtype=jnp.int32)
    ang = ((j[:, None] * j[None, :]) % n).astype(F32) * (2.0 * math.pi / n)
    return (jnp.cos(ang) * n ** -0.5).astype(BF16), (jnp.sin(ang) * n ** -0.5).astype(BF16)


def _fourier_mix(h, segments, n_groups):
    t, d = h.shape
    dg = d // n_groups
    cd, sd = _dft_mats(dg)
    tm = _tile(t, 1024, 16)
    row = pl.BlockSpec((tm, dg), lambda i, g: (i, g))
    mat = pl.BlockSpec((dg, dg), lambda i, g: (0, 0))
    yc, ys = pl.pallas_call(
        _dft_feat_kernel,
        out_shape=[jax.ShapeDtypeStruct((t, d), BF16)] * 2,
        grid=(t // tm, n_groups),
        in_specs=[row, mat, mat],
        out_specs=[row, row],
        compiler_params=_params(("parallel", "parallel")),
        name="dft_features")(h, cd, sd)
    outs = []
    for row0, nb, s in segments:
        cs, ss = _dft_mats(s)
        tms, tks, tn = _tile(s, 1024, 16), _tile(s, 1024, LANES), _tile(d, 1024, LANES)
        assert row0 % tms == 0 and row0 % tks == 0
        ni, nk = s // tms, s // tks
        a_spec = pl.BlockSpec((tms, tks), lambda b, i, j, k: (i, k))
        y_spec = pl.BlockSpec((tks, tn), lambda b, i, j, k: (row0 // tks + b * nk + k, j))
        outs.append(pl.pallas_call(
            functools.partial(_dft_seq_kernel, nk=nk),
            out_shape=jax.ShapeDtypeStruct((nb * s, d), BF16),
            grid=(nb, ni, d // tn, nk),
            in_specs=[a_spec, a_spec, y_spec, y_spec],
            out_specs=pl.BlockSpec((tms, tn), lambda b, i, j, k: (b * ni + i, j)),
            scratch_shapes=[pltpu.VMEM((tms, tn), F32)],
            compiler_params=_params(("parallel", "parallel", "parallel", "arbitrary")),
            name="dft_tokens")(cs, ss, yc, ys))
    return jnp.concatenate(outs, axis=0)


def _router_kernel(h_ref, w_ref, b_ref, idx_ref, gate_ref):
    logits = jnp.dot(_bf16(h_ref[...]), w_ref[...], preferred_element_type=F32) + b_ref[...]
    lane = lax.broadcasted_iota(jnp.int32, logits.shape, 1)
    v1 = jnp.max(logits, axis=-1, keepdims=True)
    i1 = jnp.min(jnp.where(logits == v1, lane, LANES), axis=-1, keepdims=True)
    rest = jnp.where(lane == i1, -jnp.inf, logits)
    v2 = jnp.max(rest, axis=-1, keepdims=True)
    i2 = jnp.min(jnp.where(rest == v2, lane, LANES), axis=-1, keepdims=True)
    e2 = jnp.exp(v2 - v1)
    g1 = 1.0 / (1.0 + e2)
    g2 = e2 / (1.0 + e2)
    idx_ref[...] = jnp.where(lane == 0, i1, jnp.where(lane == 1, i2, 0))
    gate_ref[...] = jnp.where(lane == 0, g1, jnp.where(lane == 1, g2, 0.0))


def _router(h, w_router, b_router, tm):
    t, d = h.shape
    ne = w_router.shape[1]
    w = jnp.pad(w_router, ((0, 0), (0, LANES - ne))).astype(BF16)
    b = jnp.concatenate([b_router.astype(F32), jnp.full((LANES - ne,), -jnp.inf, F32)]).reshape(1, LANES)
    out = pl.BlockSpec((tm, LANES), lambda i: (i, 0))
    return pl.pallas_call(
        _router_kernel,
        out_shape=[jax.ShapeDtypeStruct((t, LANES), jnp.int32), jax.ShapeDtypeStruct((t, LANES), F32)],
        grid=(t // tm,),
        in_specs=[pl.BlockSpec((tm, d), lambda i: (i, 0)),
                  pl.BlockSpec((d, LANES), lambda i: (0, 0)),
                  pl.BlockSpec((1, LANES), lambda i: (0, 0))],
        out_specs=[out, out],
        compiler_params=_params(("parallel",)),
        name="router")(h, w, b)


def _gather_kernel(idx_ref, src_ref, o_ref, sem, *, rows):
    base = pl.program_id(0) * rows

    def copy(r, src_row):
        return pltpu.make_async_copy(src_ref.at[pl.ds(src_row, 1), :], o_ref.at[pl.ds(r, 1), :], sem)

    def issue(r, carry):
        copy(r, idx_ref[base + r]).start()
        return carry

    def drain(r, carry):
        copy(r, 0).wait()
        return carry

    lax.fori_loop(0, rows, issue, 0)
    lax.fori_loop(0, rows, drain, 0)


def _gather_rows(src, idx, rows):
    p = idx.shape[0]
    d = src.shape[1]
    rows = _tile(p, rows, 8)
    return pl.pallas_call(
        functools.partial(_gather_kernel, rows=rows),
        out_shape=jax.ShapeDtypeStruct((p, d), src.dtype),
        grid_spec=pltpu.PrefetchScalarGridSpec(
            num_scalar_prefetch=1, grid=(p // rows,),
            in_specs=[pl.BlockSpec(memory_space=pl.ANY)],
            out_specs=pl.BlockSpec((rows, d), lambda i, idx: (i, 0)),
            scratch_shapes=[pltpu.SemaphoreType.DMA]),
        compiler_params=_params(("arbitrary",)),
        name="gather_rows")(idx, src)


def _dispatch_plan(top_idx, n_experts, tm):
    t, k = top_idx.shape
    flat = top_idx.reshape(-1)
    onehot = (flat[:, None] == jnp.arange(n_experts, dtype=jnp.int32)[None, :]).astype(jnp.int32)
    rank = jnp.sum((jnp.cumsum(onehot, axis=0) - 1) * onehot, axis=1)
    counts = jnp.sum(onehot, axis=0)
    padded = ((counts + tm - 1) // tm) * tm
    ends = jnp.cumsum(padded)
    starts = ends - padded
    pos = starts[flat] + rank
    p_total = ((t * k + n_experts * (tm - 1)) // tm) * tm
    src_token = jnp.zeros((p_total,), jnp.int32).at[pos].set(jnp.arange(t * k, dtype=jnp.int32) // k)
    tile_start = jnp.arange(p_total // tm, dtype=jnp.int32) * tm
    tile_expert = jnp.minimum(jnp.sum((tile_start[:, None] >= ends[None, :]).astype(jnp.int32), axis=1),
                              n_experts - 1).astype(jnp.int32)
    tile_valid = (tile_start < ends[-1]).astype(jnp.int32)
    return src_token, pos.reshape(t, k).astype(jnp.int32), tile_expert, tile_valid


def _pad_cols(w, n):
    return jnp.pad(w, [(0, 0)] * (w.ndim - 1) + [(0, n - w.shape[-1])])


def _pad_rows(w, n):
    return jnp.pad(w, [(0, 0)] * (w.ndim - 2) + [(0, n - w.shape[-2]), (0, 0)])


def _round_up(n, m):
    return ((n + m - 1) // m) * m


def kernel(x_prompt, x_sample, cache_na_k, cache_na_v, cache_mla_ckv, cache_mla_kpe, c, c_ctx, w_ada, b_ada, ln_g, ln_b, na_w_qkv, na_w_o, na_rel_bias, mla_w_dq, mla_q_norm, mla_w_uq, mla_w_dkv, mla_kv_norm, mla_w_uk, mla_w_uv, mla_w_o, fnet_w_o, ffn_w_gate, ffn_w_up, ffn_w_down, moe_w_router, moe_b_router, moe_w_gate, moe_w_up, moe_w_down):
    nbc, seq, d = x_prompt.shape
    nbl, n_lat, _ = x_sample.shape
    depth = w_ada.shape[0]
    alpha = (2 * depth) ** 0.25
    tc, tl = nbc * seq, nbl * n_lat
    t = tc + tl
    past = cache_na_k.shape[2]
    na_heads, na_hd = cache_na_k.shape[3], cache_na_k.shape[4]
    kv_lora, qk_rope = cache_mla_ckv.shape[-1], cache_mla_kpe.shape[-1]
    mla_heads, qk_dim = mla_w_uq.shape[2], mla_w_uq.shape[3]
    qk_nope = qk_dim - qk_rope
    v_dim = mla_w_uv.shape[-1]
    n_experts = moe_w_router.shape[-1]
    fnet_groups = 8
    assert qk_nope == LANES and v_dim == LANES and na_hd == LANES and qk_rope <= LANES // 2
    hq = 2 * LANES

    tm_tok = _tile(math.gcd(tc, n_lat), 128, 8)
    tile_group = jnp.asarray([0 if i * tm_tok < tc else 1 + (i * tm_tok - tc) // n_lat
                              for i in range(t // tm_tok)], jnp.int32)

    x = jnp.concatenate([x_prompt.reshape(tc, d), x_sample.reshape(tl, d)], axis=0)
    n_cond = 1 + nbl
    cond = jnp.concatenate([c_ctx[None, :], c, jnp.zeros((_round_up(n_cond, 8) - n_cond, d), F32)], axis=0)
    mod = _adaln(cond, w_ada, b_ada)
    mod = mod.reshape(depth, cond.shape[0], 6, 1, d).transpose(0, 2, 1, 3, 4)

    h = _modulate(x, mod[0, 0], mod[0, 1], tile_group, tm_tok, BF16)
    na_k, na_v, mla_ckv, mla_kpe = [], [], [], []

    for l in range(depth):
        kind, j = l % N_MIXERS, l // N_MIXERS
        if kind == 0:
            qkv = _matmul(h, [_cast_weights(na_w_qkv, j, 1)], out_dtype=F32, tm=1024, tn=1024, tk=4096, name="na_qkv")
            na_k.append(qkv[:tc, d:2 * d].reshape(nbc, seq, na_heads, na_hd))
            na_v.append(qkv[:tc, 2 * d:].reshape(nbc, seq, na_heads, na_hd))
            o_ctx = _attention(qkv, qkv, qkv, nb=nbc, nh=na_heads, sq=seq, sk=seq, dk=na_hd, dv=na_hd,
                               scale=na_hd ** -0.5, k_col0=na_heads, v_col0=2 * na_heads, heads=CTX_HEADS,
                               name="na_context")
            o_lat = _na_latent(qkv, tc, cache_na_k[:, j].reshape(nbl * past, d),
                               cache_na_v[:, j].reshape(nbl * past, d), na_rel_bias[j],
                               nb=nbl, n=n_lat, nh=na_heads, hd=na_hd)
            w_o = _cast_weights(na_w_o, j, 1)
        elif kind == 1:
            scale = qk_dim ** -0.5
            cq = _matmul(h, [_cast_weights(mla_w_dq, j, 1)], out_dtype=BF16, tm=1024, tn=mla_w_dq.shape[-1], tk=4096,
                         epilogue=functools.partial(_rms_epilogue, mla_w_dq.shape[-1]),
                         extras=[mla_q_norm[j].reshape(1, -1)], name="mla_dq")
            w_uq = _pad_cols(mla_w_uq[j], hq).reshape(-1, mla_heads * hq).astype(BF16)
            tm_q = _tile(math.gcd(tc, n_lat), 1024, 16)
            tn_q = _tile(mla_heads * hq, 1024, hq)
            ident = (jnp.ones((tm_q, hq), F32), jnp.zeros((tm_q, hq), F32), jnp.zeros((tm_q, hq), F32))
            q_tabs = [jnp.tile(jnp.concatenate([i_tab, r_tab], axis=0), (1, tn_q // hq))
                      for i_tab, r_tab in zip(ident, _rope_tables(n_lat, hq, qk_nope, qk_rope))]
            q = _matmul(cq, [w_uq], out_dtype=BF16, tm=tm_q, tn=tn_q, tk=4096,
                        epilogue=functools.partial(_rope_scale_epilogue, scale * LOG2E), row_tables=q_tabs,
                        row_block=lambda i: jnp.where(i < tc // tm_q, 0, 1 + (i - tc // tm_q) % (n_lat // tm_q)),
                        name="mla_uq")
            kvw = _round_up(kv_lora + qk_rope, LANES)
            kv = _matmul(h, [_pad_cols(mla_w_dkv[j], kvw).astype(BF16)], out_dtype=F32, tm=1024, tn=kvw, tk=4096,
                         epilogue=functools.partial(_rms_epilogue, kv_lora),
                         extras=[_pad_cols(mla_kv_norm[j].reshape(1, -1), kvw)], name="mla_dkv")
            mla_ckv.append(kv[:tc, :kv_lora].reshape(nbc, seq, kv_lora))
            mla_kpe.append(kv[:tc, kv_lora:kv_lora + qk_rope].reshape(nbc, seq, qk_rope))
            kpe_lat = _rope_rows(kv, kv_lora // LANES, tc, nbl, n_lat, _rope_tables(n_lat, LANES, 0, qk_rope))
            ckv_lat = jnp.concatenate([kv[tc:, :kv_lora], kpe_lat], axis=1).reshape(nbl, n_lat, kvw)
            ckv_past = _pad_cols(jnp.concatenate([cache_mla_ckv[:, j], cache_mla_kpe[:, j]], axis=-1), kvw)
            ckv_all = jnp.concatenate([ckv_lat, ckv_past], axis=1).reshape(nbl * (n_lat + past), kvw).astype(BF16)
            ckv_ctx = kv[:tc].astype(BF16)
            eye = jnp.pad(jnp.eye(qk_rope, dtype=F32), ((0, kvw - kv_lora - qk_rope), (0, 0)))
            w_k = jnp.concatenate([
                jnp.concatenate([mla_w_uk[j], jnp.zeros((kvw - kv_lora, mla_heads, qk_nope), F32)], axis=0),
                jnp.broadcast_to(jnp.concatenate([jnp.zeros((kv_lora, qk_rope), F32), eye], axis=0)[:, None, :],
                                 (kvw, mla_heads, qk_rope)),
                jnp.zeros((kvw, mla_heads, hq - qk_dim), F32)], axis=-1).reshape(kvw, mla_heads * hq).astype(BF16)
            w_v = _pad_rows(mla_w_uv[j].reshape(kv_lora, mla_heads * v_dim), kvw).astype(BF16)
            k_ctx = _matmul(ckv_ctx, [w_k], out_dtype=BF16, tm=1024, tn=1024, tk=kvw, name="mla_k_ctx")
            v_ctx = _matmul(ckv_ctx, [w_v], out_dtype=BF16, tm=1024, tn=1024, tk=kvw, name="mla_v_ctx")
            k_lat = _matmul(ckv_all, [w_k], out_dtype=BF16, tm=1024, tn=1024, tk=kvw, name="mla_k_lat")
            v_lat = _matmul(ckv_all, [w_v], out_dtype=BF16, tm=1024, tn=1024, tk=kvw, name="mla_v_lat")
            o_ctx = _attention(q, k_ctx, v_ctx, nb=nbc, nh=mla_heads, sq=seq, sk=seq, dk=hq, dv=v_dim,
                               scale=None, heads=CTX_HEADS, name="mla_context")
            o_lat = _attention(q, k_lat, v_lat, nb=nbl, nh=mla_heads, sq=n_lat, sk=n_lat + past, dk=hq, dv=v_dim,
                               scale=None, q_row0=tc, name="mla_latent")
            w_o = _cast_weights(mla_w_o, j, 1)
        else:
            o_ctx = None
            o_lat = _fourier_mix(h, [(0, nbc, seq), (tc, nbl, n_lat)], fnet_groups)
            w_o = _cast_weights(fnet_w_o, j, 1)
        o = o_lat if o_ctx is None else jnp.concatenate([o_ctx, o_lat], axis=0)
        y = _matmul(o, [w_o], out_dtype=F32, tm=1024, tn=1024, tk=4096, name="mixer_out")

        jc = l // 2
        moe = l % 2 == 1
        x, h = _residual_ln(x, [y], mod[l, 2], ln_g[l, 0], ln_b[l, 0], (mod[l, 3], mod[l, 4]), tile_group, tm_tok,
                            alpha, h_dtype=F32 if moe else BF16)
        nxt = (mod[l + 1, 0], mod[l + 1, 1]) if l + 1 < depth else None
        if not moe:
            dff = _round_up(ffn_w_gate.shape[-1], 1024)
            u = _matmul(h, [_cast_weights(ffn_w_gate, jc, 1, cols_out=dff), _cast_weights(ffn_w_up, jc, 1, cols_out=dff)],
                        out_dtype=BF16, tm=1024, tn=512, tk=4096, epilogue=_swiglu_epilogue, name="ffn_up")
            y = _matmul(u, [_cast_weights(ffn_w_down, jc, 1, rows_out=dff)], out_dtype=F32, tm=1024, tn=1024, tk=2816,
                        name="ffn_down")
            x, h = _residual_ln(x, [y], mod[l, 5], ln_g[l, 1], ln_b[l, 1], nxt, tile_group, tm_tok, alpha)
        else:
            tm_e = 512
            dfe = _round_up(moe_w_gate.shape[-1], 512)
            idx, gates = _router(h, moe_w_router[jc], moe_b_router[jc], tm_tok)
            src_token, slot_pos, tile_expert, tile_valid = _dispatch_plan(idx[:, :TOP_K], n_experts, tm_e)
            hs = _gather_rows(h, src_token, 256)
            group = (tile_expert, tile_valid)
            u = _matmul(hs, [_cast_weights(moe_w_gate.reshape((-1,) + moe_w_gate.shape[2:]), jc * n_experts, n_experts, cols_out=dfe),
                         _cast_weights(moe_w_up.reshape((-1,) + moe_w_up.shape[2:]), jc * n_experts, n_experts, cols_out=dfe)],
                        out_dtype=BF16, tm=tm_e, tn=512, tk=4096, epilogue=_swiglu_epilogue, group=group,
                        name="moe_up")
            ys = _matmul(u, [_cast_weights(moe_w_down.reshape((-1,) + moe_w_down.shape[2:]), jc * n_experts, n_experts, rows_out=dfe)], out_dtype=F32, tm=tm_e, tn=1024, tk=dfe,
                         group=group, name="moe_down")
            y0 = _gather_rows(ys, slot_pos[:, 0], 256)
            y1 = _gather_rows(ys, slot_pos[:, 1], 256)
            x, h = _residual_ln(x, [y0, y1], mod[l, 5], ln_g[l, 1], ln_b[l, 1], nxt, tile_group, tm_tok, alpha,
                                gates=gates)

    y_prompt = x[:tc].reshape(nbc, seq, d)
    y_sample = x[tc:].reshape(nbl, n_lat, d)
    return (y_prompt, y_sample, jnp.stack(na_k, axis=1), jnp.stack(na_v, axis=1),
            jnp.stack(mla_ckv, axis=1), jnp.stack(mla_kpe, axis=1))
```

```python
import functools
import math

import numpy as np
import jax
import jax.numpy as jnp
from jax import lax
from jax.experimental import pallas as pl
from jax.experimental.pallas import tpu as pltpu

F32 = jnp.float32
BF16 = jnp.bfloat16

LANES = 128
VMEM_LIMIT = 56 * 1024 * 1024

GRID_W = 64
N_MIXERS = 3
TOP_K = 2
ROPE_BASE = 10000.0
LN_EPS = 1e-5
RMS_EPS = 1e-6
ROPE_SWAP = 16
LOG2E = 1.4426950408889634
ATTN_TQ = 512
ATTN_CK = 1024
CTX_HEADS = 4


def _tile(dim, pref, align):
    t = (min(pref, dim) // align) * align
    while t >= align:
        if dim % t == 0:
            return t
        t -= align
    return dim


def _params(sem):
    return pltpu.CompilerParams(dimension_semantics=sem, vmem_limit_bytes=VMEM_LIMIT)


def _bf16(v):
    return v if v.dtype == BF16 else v.astype(BF16)


def _mm_kernel(*refs, nk, n_b, n_extra, grouped, epilogue):
    if grouped:
        valid_ref = refs[1]
        refs = refs[2:]
    a_ref = refs[0]
    b_refs = refs[1:1 + n_b]
    extra_refs = refs[1 + n_b:1 + n_b + n_extra]
    o_ref = refs[1 + n_b + n_extra]
    acc_refs = refs[2 + n_b + n_extra:]

    def finish(accs):
        extras = [e[...] for e in extra_refs]
        o_ref[...] = epilogue(*accs, *extras).astype(o_ref.dtype)

    def compute():
        a = _bf16(a_ref[...])
        parts = [jnp.dot(a, b[...], preferred_element_type=F32) for b in b_refs]
        if nk == 1:
            finish(parts)
            return
        k = pl.program_id(2)

        @pl.when(k == 0)
        def _():
            for acc, p in zip(acc_refs, parts):
                acc[...] = p

        @pl.when(k > 0)
        def _():
            for acc, p in zip(acc_refs, parts):
                acc[...] += p

        @pl.when(k == nk - 1)
        def _():
            finish([acc[...] for acc in acc_refs])

    if not grouped:
        compute()
        return
    n_valid = valid_ref[pl.program_id(0)]
    half = a_ref.shape[0] // 2
    if nk > 1:
        pl.when(n_valid > 0)(compute)
    else:
        pl.when(n_valid > half)(compute)

        @pl.when(jnp.logical_and(n_valid > 0, n_valid <= half))
        def _():
            a = _bf16(a_ref[:half, :])
            parts = [jnp.dot(a, b[...], preferred_element_type=F32) for b in b_refs]
            extras = [e[...] for e in extra_refs]
            o_ref[:half, :] = epilogue(*parts, *extras).astype(o_ref.dtype)
            o_ref[half:, :] = jnp.zeros((o_ref.shape[0] - half, o_ref.shape[1]), o_ref.dtype)

    @pl.when(jnp.logical_and(n_valid == 0, pl.program_id(2) == nk - 1))
    def _():
        o_ref[...] = jnp.zeros(o_ref.shape, o_ref.dtype)


def _identity(acc):
    return acc


def _matmul(a, bs, *, out_dtype, tm, tn, tk, epilogue=_identity, extras=(), row_tables=(), row_block=None,
            group=None, name="mm"):
    m, kdim = a.shape
    n = bs[0].shape[-1]
    tm, tn, tk = _tile(m, tm, 16), _tile(n, tn, LANES), _tile(kdim, tk, LANES)
    nk = kdim // tk
    grouped = group is not None
    if grouped:
        b_spec = pl.BlockSpec((None, tk, tn), lambda i, j, k, gid, valid: (gid[i], k, j))
    else:
        b_spec = pl.BlockSpec((tk, tn), lambda i, j, k, *_: (k, j))
    in_specs = [pl.BlockSpec((tm, tk), lambda i, j, k, *_: (i, k))]
    in_specs += [b_spec] * len(bs)
    in_specs += [pl.BlockSpec((1, tn), lambda i, j, k, *_: (0, j))] * len(extras)
    in_specs += [pl.BlockSpec((tm, tn), lambda i, j, k, *_: (row_block(i), 0))] * len(row_tables)
    extras = list(extras) + list(row_tables)
    scratch = [pltpu.VMEM((tm, tn), F32) for _ in bs] if nk > 1 else []
    kern = functools.partial(_mm_kernel, nk=nk, n_b=len(bs), n_extra=len(extras), grouped=grouped,
                             epilogue=epilogue)
    call = pl.pallas_call(
        kern,
        out_shape=jax.ShapeDtypeStruct((m, n), out_dtype),
        grid_spec=pltpu.PrefetchScalarGridSpec(
            num_scalar_prefetch=2 if grouped else 0,
            grid=(m // tm, n // tn, nk),
            in_specs=in_specs,
            out_specs=pl.BlockSpec((tm, tn), lambda i, j, k, *_: (i, j)),
            scratch_shapes=scratch),
        compiler_params=_params(("parallel", "parallel", "arbitrary")),
        name=name)
    prefetch = tuple(group) if grouped else ()
    return call(*prefetch, a, *bs, *extras)


def _cast_kernel(w_ref, o_ref, *, rows, cols):
    tr, tc = o_ref.shape
    r = lax.broadcasted_iota(jnp.int32, (tr, tc), 0) + pl.program_id(1) * tr
    c = lax.broadcasted_iota(jnp.int32, (tr, tc), 1) + pl.program_id(2) * tc
    o_ref[...] = jnp.where(jnp.logical_and(r < rows, c < cols), w_ref[...], 0.0).astype(o_ref.dtype)


def _cast_weights(w, first, count, rows_out=None, cols_out=None):
    _, rows, cols = w.shape
    rows_out, cols_out = rows_out or rows, cols_out or cols
    tc = _tile(cols_out, 1024, LANES)
    tr = _tile(rows_out, (1 << 20) // tc, 16)
    out = pl.pallas_call(
        functools.partial(_cast_kernel, rows=rows, cols=cols),
        out_shape=jax.ShapeDtypeStruct((count, rows_out, cols_out), BF16),
        grid=(count, rows_out // tr, cols_out // tc),
        in_specs=[pl.BlockSpec((None, tr, tc), lambda g, i, j: (first + g, i, j))],
        out_specs=pl.BlockSpec((None, tr, tc), lambda g, i, j: (g, i, j)),
        compiler_params=_params(("parallel", "parallel", "parallel")),
        name="cast_weights")(w)
    return out.reshape(rows_out, cols_out) if count == 1 else out


def _swiglu_epilogue(g, u):
    return jax.nn.silu(g) * u


def _rope_scale_epilogue(scale, acc, cos, sin_up, sin_dn):
    return _rope(acc, cos, sin_up, sin_dn) * scale


def _rms_epilogue(n_norm, acc, gain):
    width = acc.shape[-1]
    if n_norm == width:
        ms = jnp.mean(jnp.square(acc), axis=-1, keepdims=True)
        return acc * lax.rsqrt(ms + RMS_EPS) * gain
    lane = lax.broadcasted_iota(jnp.int32, acc.shape, 1)
    head = lane < n_norm
    ms = jnp.sum(jnp.where(head, jnp.square(acc), 0.0), axis=-1, keepdims=True) / n_norm
    return jnp.where(head, acc * lax.rsqrt(ms + RMS_EPS) * gain, acc)


def _ada_kernel(c_ref, w_ref, b_ref, o_ref):
    s = jax.nn.silu(c_ref[...]).astype(BF16)
    o_ref[...] = jnp.dot(s, w_ref[...].astype(BF16), preferred_element_type=F32) + b_ref[...]


def _adaln(cond, w_ada, b_ada):
    depth, d, n = w_ada.shape
    g = cond.shape[0]
    tn = _tile(n, 512, LANES)
    return pl.pallas_call(
        _ada_kernel,
        out_shape=jax.ShapeDtypeStruct((depth, g, n), F32),
        grid=(depth, n // tn),
        in_specs=[pl.BlockSpec((g, d), lambda l, j: (0, 0)),
                  pl.BlockSpec((None, d, tn), lambda l, j: (l, 0, j)),
                  pl.BlockSpec((None, 1, tn), lambda l, j: (l, 0, j))],
        out_specs=pl.BlockSpec((None, g, tn), lambda l, j: (l, 0, j)),
        compiler_params=_params(("parallel", "parallel")),
        name="adaln")(cond, w_ada, b_ada.reshape(depth, 1, n))


def _modulate_kernel(gid_ref, x_ref, sh_ref, sc_ref, h_ref):
    h_ref[...] = (x_ref[...] * (1.0 + sc_ref[...]) + sh_ref[...]).astype(h_ref.dtype)


def _modulate(x, shift, scale, tile_group, tm, out_dtype):
    t, d = x.shape
    row = pl.BlockSpec((tm, d), lambda i, gid: (i, 0))
    mod = pl.BlockSpec((None, 1, d), lambda i, gid: (gid[i], 0, 0))
    return pl.pallas_call(
        _modulate_kernel,
        out_shape=jax.ShapeDtypeStruct((t, d), out_dtype),
        grid_spec=pltpu.PrefetchScalarGridSpec(
            num_scalar_prefetch=1, grid=(t // tm,), in_specs=[row, mod, mod], out_specs=row),
        compiler_params=_params(("parallel",)),
        name="modulate")(tile_group, x, shift, scale)


def _ln_kernel(gid_ref, *refs, alpha, n_y, emit_h):
    x_ref = refs[0]
    if n_y == 1:
        y = refs[1][...]
        refs = refs[2:]
    else:
        gates = refs[3][...]
        y = gates[:, 0:1] * refs[1][...] + gates[:, 1:2] * refs[2][...]
        refs = refs[4:]
    g_ref, lg_ref, lb_ref = refs[:3]
    z = alpha * x_ref[...] + g_ref[...] * y
    mu = jnp.mean(z, axis=-1, keepdims=True)
    zc = z - mu
    var = jnp.mean(jnp.square(zc), axis=-1, keepdims=True)
    xn = zc * lax.rsqrt(var + LN_EPS) * lg_ref[...] + lb_ref[...]
    if emit_h:
        sh_ref, sc_ref, xo_ref, ho_ref = refs[3:]
        xo_ref[...] = xn
        ho_ref[...] = (xn * (1.0 + sc_ref[...]) + sh_ref[...]).astype(ho_ref.dtype)
    else:
        refs[3][...] = xn


def _residual_ln(x, ys, gate, ln_g, ln_b, nxt, tile_group, tm, alpha, h_dtype=BF16, gates=None):
    t, d = x.shape
    row = pl.BlockSpec((tm, d), lambda i, gid: (i, 0))
    mod = pl.BlockSpec((None, 1, d), lambda i, gid: (gid[i], 0, 0))
    vec = pl.BlockSpec((1, d), lambda i, gid: (0, 0))
    args = [x, *ys]
    in_specs = [row] * len(args)
    if gates is not None:
        args.append(gates)
        in_specs.append(pl.BlockSpec((tm, LANES), lambda i, gid: (i, 0)))
    args += [gate, ln_g.reshape(1, d), ln_b.reshape(1, d)]
    in_specs += [mod, vec, vec]
    out_shape = [jax.ShapeDtypeStruct((t, d), F32)]
    out_specs = [row]
    if nxt is not None:
        args += list(nxt)
        in_specs += [mod, mod]
        out_shape.append(jax.ShapeDtypeStruct((t, d), h_dtype))
        out_specs.append(row)
    kern = functools.partial(_ln_kernel, alpha=alpha, n_y=len(ys), emit_h=nxt is not None)
    out = pl.pallas_call(
        kern,
        out_shape=out_shape,
        grid_spec=pltpu.PrefetchScalarGridSpec(
            num_scalar_prefetch=1, grid=(t // tm,), in_specs=in_specs, out_specs=out_specs),
        compiler_params=_params(("parallel",)),
        name="residual_ln")(tile_group, *args)
    return (out[0], out[1]) if nxt is not None else (out[0], None)


def _rope(x, cos, sin_up, sin_dn):
    w = x.shape[-1]
    return x * cos + pltpu.roll(x, w - ROPE_SWAP, 1) * sin_up + pltpu.roll(x, ROPE_SWAP, 1) * sin_dn


def _attn_kernel(q_ref, k_ref, v_ref, o_ref, *scratch, scale, sk, ck, heads, dk, dv):
    acc_ref = scratch[-1]
    nt_dims = (((1,), (1,)), ((), ()))
    tn_dims = (((0,), (0,)), ((), ()))
    c0 = sk % ck if sk % ck else ck
    chunks = [(0, c0)] + [(c0 + i * ck, ck) for i in range((sk - c0) // ck)]

    def one_head(kcols, vcols):
        if scale is None:
            def q_tile():
                return q_ref[:, kcols]
        else:
            qs_ref = scratch[0]
            qs_ref[...] = (q_ref[:, kcols] * (scale * LOG2E)).astype(BF16)

            def q_tile():
                return qs_ref[...]

        def scores(start, size):
            return lax.dot_general(_bf16(k_ref[pl.ds(start, size), kcols]), q_tile(), nt_dims,
                                   preferred_element_type=F32)

        def pv(p, start, size):
            return lax.dot_general(_bf16(v_ref[pl.ds(start, size), vcols]), p.astype(BF16), tn_dims,
                                   preferred_element_type=F32)

        s_next = scores(*chunks[0])
        m = l = None
        for idx, (start, size) in enumerate(chunks):
            s = s_next
            if idx + 1 < len(chunks):
                s_next = scores(*chunks[idx + 1])
            if idx == 0:
                m = jnp.max(s, axis=0, keepdims=True)
                p = jnp.exp2(s - m)
                l = jnp.sum(p, axis=0, keepdims=True)
                acc_ref[...] = pv(p, start, size)
            else:
                m_new = jnp.maximum(m, jnp.max(s, axis=0, keepdims=True))
                a = jnp.exp2(m - m_new)
                p = jnp.exp2(s - m_new)
                l = a * l + jnp.sum(p, axis=0, keepdims=True)
                acc_ref[...] = a * acc_ref[...] + pv(p, start, size)
                m = m_new
        o_ref[:, vcols] = (acc_ref[...] / l).T.astype(o_ref.dtype)

    for hh in range(heads):
        one_head(pl.ds(hh * dk, dk), pl.ds(hh * dv, dv))


def _skip_ref(kern, idx):
    def wrapped(*refs):
        return kern(*refs[:idx], *refs[idx + 1:])
    return wrapped


def _into(kern, args, in_specs, base):
    if base is None:
        return kern, args, in_specs, {}
    idx = len(args)
    return (_skip_ref(kern, idx), args + [base], in_specs + [pl.BlockSpec(memory_space=pl.ANY)], {idx: 0})


def _attention(q_arr, k_arr, v_arr, *, nb, nh, sq, sk, dk, dv, scale, q_row0=0, k_row0=0,
               q_col0=0, k_col0=0, v_col0=0, heads=1, out_rows=None, out_row0=0, base=None, name="attention"):
    tq = _tile(sq, ATTN_TQ, 16)
    ck = min(ATTN_CK, sk)
    assert q_row0 % tq == 0 and k_row0 % sk == 0 and (sk % ck) % LANES == 0 and out_row0 % tq == 0
    assert nh % heads == 0 and q_col0 % heads == 0 and k_col0 % heads == 0 and v_col0 % heads == 0
    nq = sq // tq
    scratch = [pltpu.VMEM((dv, tq), F32)]
    if scale is not None:
        scratch.insert(0, pltpu.VMEM((tq, dk), BF16))
    kern = functools.partial(_attn_kernel, scale=scale, sk=sk, ck=ck, heads=heads, dk=dk, dv=dv)
    in_specs = [pl.BlockSpec((tq, heads * dk), lambda b, h, i: (q_row0 // tq + b * nq + i, q_col0 // heads + h)),
                pl.BlockSpec((sk, heads * dk), lambda b, h, i: (k_row0 // sk + b, k_col0 // heads + h)),
                pl.BlockSpec((sk, heads * dv), lambda b, h, i: (k_row0 // sk + b, v_col0 // heads + h))]
    kern, args, in_specs, aliases = _into(kern, [q_arr, k_arr, v_arr], in_specs, base)
    return pl.pallas_call(
        kern,
        out_shape=jax.ShapeDtypeStruct((out_rows or nb * sq, nh * dv), BF16),
        grid=(nb, nh // heads, nq),
        in_specs=in_specs,
        out_specs=pl.BlockSpec((tq, heads * dv), lambda b, h, i: (out_row0 // tq + b * nq + i, h)),
        scratch_shapes=scratch,
        input_output_aliases=aliases,
        compiler_params=_params(("parallel", "parallel", "parallel")),
        name=name)(*args)


NA_TILE_ROWS = 4
NA_KEY_ROWS = 12


def _na_tiling(rows, win_h):
    assert rows % NA_TILE_ROWS == 0 and rows >= NA_KEY_ROWS and win_h == 8
    key_start, case_of_tile, cases = [], [], []
    for t in range(rows // NA_TILE_ROWS):
        ks = int(np.clip(NA_TILE_ROWS * t - win_h // 2, 0, rows - NA_KEY_ROWS))
        idx = np.full((NA_TILE_ROWS, NA_KEY_ROWS), 2 * win_h - 1, np.int64)
        for qi in range(NA_TILE_ROWS):
            r = NA_TILE_ROWS * t + qi
            r0 = int(np.clip(r - win_h // 2, 0, rows - win_h))
            for kj in range(r0 - ks, r0 - ks + win_h):
                idx[qi, kj] = ks + kj - r + win_h - 1
        keys = [c.tobytes() for c in cases]
        if idx.tobytes() not in keys:
            cases.append(idx)
            keys.append(idx.tobytes())
        key_start.append(ks)
        case_of_tile.append(keys.index(idx.tobytes()))
    return key_start, case_of_tile, cases


def _na_kernel(q_ref, k_ref, v_ref, kc_ref, vc_ref, tab_ref, o_ref, qb_ref, kb_ref, vb_ref, *,
               key_start, case_of_tile, scale):
    qb_ref[...] = (q_ref[...] * (scale * LOG2E)).astype(BF16)
    kb_ref[...] = k_ref[...].astype(BF16)
    vb_ref[...] = v_ref[...].astype(BF16)
    kc = _bf16(kc_ref[...])
    vc = _bf16(vc_ref[...])
    nq = NA_TILE_ROWS * GRID_W
    nk = NA_KEY_ROWS * GRID_W
    nt_dims = (((1,), (1,)), ((), ()))
    tn_dims = (((0,), (0,)), ((), ()))

    def scores(t):
        q = qb_ref[pl.ds(t * nq, nq), :]
        s_loc = lax.dot_general(kb_ref[pl.ds(key_start[t] * GRID_W, nk), :], q, nt_dims,
                                preferred_element_type=F32) + tab_ref[case_of_tile[t]]
        s_ctx = lax.dot_general(kc, q, nt_dims, preferred_element_type=F32)
        return s_loc, s_ctx

    nxt = scores(0)
    for t in range(len(key_start)):
        s_loc, s_ctx = nxt
        if t + 1 < len(key_start):
            nxt = scores(t + 1)
        m = jnp.maximum(jnp.max(s_loc, axis=0, keepdims=True), jnp.max(s_ctx, axis=0, keepdims=True))
        p_loc = jnp.exp2(s_loc - m)
        p_ctx = jnp.exp2(s_ctx - m)
        l = jnp.sum(p_loc, axis=0, keepdims=True) + jnp.sum(p_ctx, axis=0, keepdims=True)
        o = (lax.dot_general(vb_ref[pl.ds(key_start[t] * GRID_W, nk), :], p_loc.astype(BF16), tn_dims,
                             preferred_element_type=F32)
             + lax.dot_general(vc, p_ctx.astype(BF16), tn_dims, preferred_element_type=F32))
        o_ref[pl.ds(t * nq, nq), :] = (o / l).T.astype(o_ref.dtype)


def _na_bias_table(rel_bias, cases):
    nh, n_dr, n_dc = rel_bias.shape
    win_w = (n_dc + 1) // 2
    col = np.arange(GRID_W)
    cs = np.clip(col - win_w // 2, 0, GRID_W - win_w)
    in_win = (col[None, :] >= cs[:, None]) & (col[None, :] < cs[:, None] + win_w)
    ext = jnp.pad(rel_bias.astype(F32) * LOG2E, ((0, 0), (0, 0), (GRID_W - win_w, GRID_W - win_w)))
    toe = jnp.stack([ext[:, :, GRID_W - 1 - q:2 * GRID_W - 1 - q] for q in range(GRID_W)], axis=-1)
    toe = jnp.where(jnp.asarray(in_win.T)[None, None], toe, -jnp.inf)
    toe = jnp.concatenate([toe, jnp.full((nh, 1, GRID_W, GRID_W), -jnp.inf, F32)], axis=1)
    tabs = []
    for idx in cases:
        rows_k = [jnp.concatenate([toe[:, idx[qi, kj]] for qi in range(idx.shape[0])], axis=-1)
                  for kj in range(idx.shape[1])]
        tabs.append(jnp.concatenate(rows_k, axis=1))
    return jnp.stack(tabs, axis=1)


def _na_latent(qkv, row0, k_ctx, v_ctx, rel_bias, base, *, nb, n, nh, hd):
    rows = n // GRID_W
    win_h = min((rel_bias.shape[1] + 1) // 2, rows)
    assert row0 % n == 0
    blk0 = row0 // n
    p = k_ctx.shape[0] // nb
    key_start, case_of_tile, cases = _na_tiling(rows, win_h)
    tab = _na_bias_table(rel_bias, cases)
    kern = functools.partial(_na_kernel, key_start=key_start, case_of_tile=case_of_tile, scale=hd ** -0.5)
    in_specs = [pl.BlockSpec((n, hd), lambda b, h: (blk0 + b, h)),
                pl.BlockSpec((n, hd), lambda b, h: (blk0 + b, nh + h)),
                pl.BlockSpec((n, hd), lambda b, h: (blk0 + b, 2 * nh + h)),
                pl.BlockSpec((p, hd), lambda b, h: (b, h)),
                pl.BlockSpec((p, hd), lambda b, h: (b, h)),
                pl.BlockSpec((None,) + tab.shape[1:], lambda b, h: (h, 0, 0, 0))]
    kern, args, in_specs, aliases = _into(kern, [qkv, qkv, qkv, k_ctx, v_ctx, tab], in_specs, base)
    return pl.pallas_call(
        kern,
        out_shape=jax.ShapeDtypeStruct(base.shape, BF16),
        grid=(nb, nh),
        in_specs=in_specs,
        out_specs=pl.BlockSpec((n, hd), lambda b, h: (blk0 + b, h)),
        scratch_shapes=[pltpu.VMEM((n, hd), BF16)] * 3,
        input_output_aliases=aliases,
        compiler_params=_params(("parallel", "parallel")),
        name="na_latent")(*args)


def _rope_kernel(x_ref, cos_ref, su_ref, sd_ref, o_ref):
    o_ref[...] = _rope(x_ref[...], cos_ref[...], su_ref[...], sd_ref[...])


def _rope_rows(x, col_blk, row0, nb, n, tabs):
    tr = _tile(n, 1024, 8)
    assert row0 % tr == 0
    nt = n // tr
    return pl.pallas_call(
        _rope_kernel,
        out_shape=jax.ShapeDtypeStruct((nb * n, LANES), F32),
        grid=(nb, nt),
        in_specs=[pl.BlockSpec((tr, LANES), lambda b, i: (row0 // tr + b * nt + i, col_blk))]
        + [pl.BlockSpec((tr, LANES), lambda b, i: (i, 0))] * 3,
        out_specs=pl.BlockSpec((tr, LANES), lambda b, i: (b * nt + i, 0)),
        compiler_params=_params(("parallel", "parallel")),
        name="rope_kpe")(x, *tabs)


def _rope_tables(n, width, lane0, rope_dim):
    t = jnp.arange(n)
    row = (t // GRID_W).astype(F32)
    col = (t % GRID_W).astype(F32)
    half = rope_dim // 2
    nf = half // 2
    assert nf == ROPE_SWAP
    inv_freq = ROPE_BASE ** (-jnp.arange(nf, dtype=F32) / nf)
    ang_r = row[:, None] * inv_freq[None, :]
    ang_c = col[:, None] * inv_freq[None, :]
    zeros = jnp.zeros((n, nf), F32)
    cos = jnp.concatenate([jnp.cos(ang_r)] * 2 + [jnp.cos(ang_c)] * 2, axis=1)
    sin_up = jnp.concatenate([-jnp.sin(ang_r), zeros, -jnp.sin(ang_c), zeros], axis=1)
    sin_dn = jnp.concatenate([zeros, jnp.sin(ang_r), zeros, jnp.sin(ang_c)], axis=1)

    def place(tab, fill):
        left = jnp.full((n, lane0), fill, F32)
        right = jnp.full((n, width - lane0 - rope_dim), fill, F32)
        return jnp.concatenate([left, tab, right], axis=1)

    return place(cos, 1.0), place(sin_up, 0.0), place(sin_dn, 0.0)


def _dft_feat_kernel(h_ref, c_ref, s_ref, yc_ref, ys_ref):
    h = _bf16(h_ref[...])
    yc_ref[...] = jnp.dot(h, c_ref[...], preferred_element_type=F32).astype(yc_ref.dtype)
    ys_ref[...] = jnp.dot(h, s_ref[...], preferred_element_type=F32).astype(ys_ref.dtype)


def _dft_seq_kernel(c_ref, s_ref, yc_ref, ys_ref, o_ref, acc_ref, *, nk):
    k = pl.program_id(3)
    part = (jnp.dot(c_ref[...], yc_ref[...], preferred_element_type=F32)
            - jnp.dot(s_ref[...], ys_ref[...], preferred_element_type=F32))

    @pl.when(k == 0)
    def _():
        acc_ref[...] = part

    @pl.when(k > 0)
    def _():
        acc_ref[...] += part

    @pl.when(k == nk - 1)
    def _():
        o_ref[...] = acc_ref[...].astype(o_ref.dtype)


def _dft_mats(n):
    j = jnp.arange(n, dtype=jnp.int32)
    ang = ((j[:, None] * j[None, :]) % n).astype(F32) * (2.0 * math.pi / n)
    return (jnp.cos(ang) * n ** -0.5).astype(BF16), (jnp.sin(ang) * n ** -0.5).astype(BF16)


def _fourier_mix(h, segments, n_groups):
    t, d = h.shape
    dg = d // n_groups
    cd, sd = _dft_mats(dg)
    tm = _tile(t, 1024, 16)
    row = pl.BlockSpec((tm, dg), lambda i, g: (i, g))
    mat = pl.BlockSpec((dg, dg), lambda i, g: (0, 0))
    yc, ys = pl.pallas_call(
        _dft_feat_kernel,
        out_shape=[jax.ShapeDtypeStruct((t, d), BF16)] * 2,
        grid=(t // tm, n_groups),
        in_specs=[row, mat, mat],
        out_specs=[row, row],
        compiler_params=_params(("parallel", "parallel")),
        name="dft_features")(h, cd, sd)
    out = None
    for row0, nb, s in segments:
        cs, ss = _dft_mats(s)
        tms, tks, tn = _tile(s, 1024, 16), _tile(s, 1024, LANES), _tile(d, 1024, LANES)
        assert row0 % tms == 0 and row0 % tks == 0
        ni, nk = s // tms, s // tks
        a_spec = pl.BlockSpec((tms, tks), lambda b, i, j, k: (i, k))
        y_spec = pl.BlockSpec((tks, tn), lambda b, i, j, k: (row0 // tks + b * nk + k, j))
        kern, args, in_specs, aliases = _into(functools.partial(_dft_seq_kernel, nk=nk), [cs, ss, yc, ys],
                                              [a_spec, a_spec, y_spec, y_spec], out)
        out = pl.pallas_call(
            kern,
            out_shape=jax.ShapeDtypeStruct((t, d), BF16),
            grid=(nb, ni, d // tn, nk),
            in_specs=in_specs,
            out_specs=pl.BlockSpec((tms, tn), lambda b, i, j, k: (row0 // tms + b * ni + i, j)),
            scratch_shapes=[pltpu.VMEM((tms, tn), F32)],
            input_output_aliases=aliases,
            compiler_params=_params(("parallel", "parallel", "parallel", "arbitrary")),
            name="dft_tokens")(*args)
    return out


def _router_kernel(h_ref, w_ref, b_ref, idx_ref, gate_ref):
    logits = jnp.dot(_bf16(h_ref[...]), w_ref[...], preferred_element_type=F32) + b_ref[...]
    lane = lax.broadcasted_iota(jnp.int32, logits.shape, 1)
    v1 = jnp.max(logits, axis=-1, keepdims=True)
    i1 = jnp.min(jnp.where(logits == v1, lane, LANES), axis=-1, keepdims=True)
    rest = jnp.where(lane == i1, -jnp.inf, logits)
    v2 = jnp.max(rest, axis=-1, keepdims=True)
    i2 = jnp.min(jnp.where(rest == v2, lane, LANES), axis=-1, keepdims=True)
    e2 = jnp.exp(v2 - v1)
    g1 = 1.0 / (1.0 + e2)
    g2 = e2 / (1.0 + e2)
    idx_ref[...] = jnp.where(lane == 0, i1, jnp.where(lane == 1, i2, 0))
    gate_ref[...] = jnp.where(lane == 0, g1, jnp.where(lane == 1, g2, 0.0))


def _router(h, w_router, b_router, tm):
    t, d = h.shape
    ne = w_router.shape[1]
    w = jnp.pad(w_router, ((0, 0), (0, LANES - ne))).astype(BF16)
    b = jnp.concatenate([b_router.astype(F32), jnp.full((LANES - ne,), -jnp.inf, F32)]).reshape(1, LANES)
    out = pl.BlockSpec((tm, LANES), lambda i: (i, 0))
    return pl.pallas_call(
        _router_kernel,
        out_shape=[jax.ShapeDtypeStruct((t, LANES), jnp.int32), jax.ShapeDtypeStruct((t, LANES), F32)],
        grid=(t // tm,),
        in_specs=[pl.BlockSpec((tm, d), lambda i: (i, 0)),
                  pl.BlockSpec((d, LANES), lambda i: (0, 0)),
                  pl.BlockSpec((1, LANES), lambda i: (0, 0))],
        out_specs=[out, out],
        compiler_params=_params(("parallel",)),
        name="router")(h, w, b)


def _gather_kernel(idx_ref, src_ref, o_ref, sem, *, rows):
    base = pl.program_id(0) * rows

    def copy(r, src_row):
        return pltpu.make_async_copy(src_ref.at[pl.ds(src_row, 1), :], o_ref.at[pl.ds(r, 1), :], sem)

    def issue(r, carry):
        copy(r, idx_ref[base + r]).start()
        return carry

    def drain(r, carry):
        copy(r, 0).wait()
        return carry

    lax.fori_loop(0, rows, issue, 0)
    lax.fori_loop(0, rows, drain, 0)


def _gather_rows(src, idx, rows):
    p = idx.shape[0]
    d = src.shape[1]
    rows = _tile(p, rows, 8)
    return pl.pallas_call(
        functools.partial(_gather_kernel, rows=rows),
        out_shape=jax.ShapeDtypeStruct((p, d), src.dtype),
        grid_spec=pltpu.PrefetchScalarGridSpec(
            num_scalar_prefetch=1, grid=(p // rows,),
            in_specs=[pl.BlockSpec(memory_space=pl.ANY)],
            out_specs=pl.BlockSpec((rows, d), lambda i, idx: (i, 0)),
            scratch_shapes=[pltpu.SemaphoreType.DMA]),
        compiler_params=_params(("arbitrary",)),
        name="gather_rows")(idx, src)


def _dispatch_plan(top_idx, n_experts, tm):
    t, k = top_idx.shape
    flat = top_idx.reshape(-1)
    onehot = (flat[:, None] == jnp.arange(n_experts, dtype=jnp.int32)[None, :]).astype(jnp.int32)
    rank = jnp.sum((jnp.cumsum(onehot, axis=0) - 1) * onehot, axis=1)
    counts = jnp.sum(onehot, axis=0)
    padded = ((counts + tm - 1) // tm) * tm
    ends = jnp.cumsum(padded)
    starts = ends - padded
    pos = starts[flat] + rank
    p_total = ((t * k + n_experts * (tm - 1)) // tm) * tm
    src_token = jnp.zeros((p_total,), jnp.int32).at[pos].set(jnp.arange(t * k, dtype=jnp.int32) // k)
    tile_start = jnp.arange(p_total // tm, dtype=jnp.int32) * tm
    tile_expert = jnp.minimum(jnp.sum((tile_start[:, None] >= ends[None, :]).astype(jnp.int32), axis=1),
                              n_experts - 1).astype(jnp.int32)
    group_end = (starts + counts)[tile_expert]
    tile_valid = jnp.where(tile_start < ends[-1], jnp.clip(group_end - tile_start, 0, tm), 0).astype(jnp.int32)
    return src_token, pos.reshape(t, k).astype(jnp.int32), tile_expert, tile_valid


def _pad_cols(w, n):
    return jnp.pad(w, [(0, 0)] * (w.ndim - 1) + [(0, n - w.shape[-1])])


def _pad_rows(w, n):
    return jnp.pad(w, [(0, 0)] * (w.ndim - 2) + [(0, n - w.shape[-2]), (0, 0)])


def _round_up(n, m):
    return ((n + m - 1) // m) * m


def kernel(x_prompt, x_sample, cache_na_k, cache_na_v, cache_mla_ckv, cache_mla_kpe, c, c_ctx, w_ada, b_ada, ln_g, ln_b, na_w_qkv, na_w_o, na_rel_bias, mla_w_dq, mla_q_norm, mla_w_uq, mla_w_dkv, mla_kv_norm, mla_w_uk, mla_w_uv, mla_w_o, fnet_w_o, ffn_w_gate, ffn_w_up, ffn_w_down, moe_w_router, moe_b_router, moe_w_gate, moe_w_up, moe_w_down):
    nbc, seq, d = x_prompt.shape
    nbl, n_lat, _ = x_sample.shape
    depth = w_ada.shape[0]
    alpha = (2 * depth) ** 0.25
    tc, tl = nbc * seq, nbl * n_lat
    t = tc + tl
    past = cache_na_k.shape[2]
    na_heads, na_hd = cache_na_k.shape[3], cache_na_k.shape[4]
    kv_lora, qk_rope = cache_mla_ckv.shape[-1], cache_mla_kpe.shape[-1]
    mla_heads, qk_dim = mla_w_uq.shape[2], mla_w_uq.shape[3]
    qk_nope = qk_dim - qk_rope
    v_dim = mla_w_uv.shape[-1]
    n_experts = moe_w_router.shape[-1]
    fnet_groups = 8
    assert qk_nope == LANES and v_dim == LANES and na_hd == LANES and qk_rope <= LANES // 2
    hq = 2 * LANES

    tm_tok = _tile(math.gcd(tc, n_lat), 128, 8)
    tile_group = jnp.asarray([0 if i * tm_tok < tc else 1 + (i * tm_tok - tc) // n_lat
                              for i in range(t // tm_tok)], jnp.int32)

    x = jnp.concatenate([x_prompt.reshape(tc, d), x_sample.reshape(tl, d)], axis=0)
    n_cond = 1 + nbl
    cond = jnp.concatenate([c_ctx[None, :], c, jnp.zeros((_round_up(n_cond, 8) - n_cond, d), F32)], axis=0)
    mod = _adaln(cond, w_ada, b_ada)
    mod = mod.reshape(depth, cond.shape[0], 6, 1, d).transpose(0, 2, 1, 3, 4)

    h = _modulate(x, mod[0, 0], mod[0, 1], tile_group, tm_tok, BF16)
    na_k, na_v, mla_ckv, mla_kpe = [], [], [], []

    for l in range(depth):
        kind, j = l % N_MIXERS, l // N_MIXERS
        if kind == 0:
            qkv = _matmul(h, [_cast_weights(na_w_qkv, j, 1)], out_dtype=F32, tm=1024, tn=1024, tk=4096, name="na_qkv")
            na_k.append(qkv[:tc, d:2 * d].reshape(nbc, seq, na_heads, na_hd))
            na_v.append(qkv[:tc, 2 * d:].reshape(nbc, seq, na_heads, na_hd))
            o_ctx = _attention(qkv, qkv, qkv, nb=nbc, nh=na_heads, sq=seq, sk=seq, dk=na_hd, dv=na_hd,
                               scale=na_hd ** -0.5, k_col0=na_heads, v_col0=2 * na_heads, heads=CTX_HEADS,
                               out_rows=t, name="na_context")
            o = _na_latent(qkv, tc, cache_na_k[:, j].reshape(nbl * past, d),
                           cache_na_v[:, j].reshape(nbl * past, d), na_rel_bias[j], o_ctx,
                           nb=nbl, n=n_lat, nh=na_heads, hd=na_hd)
            w_o = _cast_weights(na_w_o, j, 1)
        elif kind == 1:
            scale = qk_dim ** -0.5
            cq = _matmul(h, [_cast_weights(mla_w_dq, j, 1)], out_dtype=BF16, tm=1024, tn=mla_w_dq.shape[-1], tk=4096,
                         epilogue=functools.partial(_rms_epilogue, mla_w_dq.shape[-1]),
                         extras=[mla_q_norm[j].reshape(1, -1)], name="mla_dq")
            w_uq = _pad_cols(mla_w_uq[j], hq).reshape(-1, mla_heads * hq).astype(BF16)
            tm_q = _tile(math.gcd(tc, n_lat), 1024, 16)
            tn_q = _tile(mla_heads * hq, 1024, hq)
            ident = (jnp.ones((tm_q, hq), F32), jnp.zeros((tm_q, hq), F32), jnp.zeros((tm_q, hq), F32))
            q_tabs = [jnp.tile(jnp.concatenate([i_tab, r_tab], axis=0), (1, tn_q // hq))
                      for i_tab, r_tab in zip(ident, _rope_tables(n_lat, hq, qk_nope, qk_rope))]
            q = _matmul(cq, [w_uq], out_dtype=BF16, tm=tm_q, tn=tn_q, tk=4096,
                        epilogue=functools.partial(_rope_scale_epilogue, scale * LOG2E), row_tables=q_tabs,
                        row_block=lambda i: jnp.where(i < tc // tm_q, 0, 1 + (i - tc // tm_q) % (n_lat // tm_q)),
                        name="mla_uq")
            kvw = _round_up(kv_lora + qk_rope, LANES)
            kv = _matmul(h, [_pad_cols(mla_w_dkv[j], kvw).astype(BF16)], out_dtype=F32, tm=1024, tn=kvw, tk=4096,
                         epilogue=functools.partial(_rms_epilogue, kv_lora),
                         extras=[_pad_cols(mla_kv_norm[j].reshape(1, -1), kvw)], name="mla_dkv")
            mla_ckv.append(kv[:tc, :kv_lora].reshape(nbc, seq, kv_lora))
            mla_kpe.append(kv[:tc, kv_lora:kv_lora + qk_rope].reshape(nbc, seq, qk_rope))
            kpe_lat = _rope_rows(kv, kv_lora // LANES, tc, nbl, n_lat, _rope_tables(n_lat, LANES, 0, qk_rope))
            ckv_lat = jnp.concatenate([kv[tc:, :kv_lora], kpe_lat], axis=1).reshape(nbl, n_lat, kvw)
            ckv_past = _pad_cols(jnp.concatenate([cache_mla_ckv[:, j], cache_mla_kpe[:, j]], axis=-1), kvw)
            ckv_all = jnp.concatenate([ckv_lat, ckv_past], axis=1).reshape(nbl * (n_lat + past), kvw).astype(BF16)
            ckv_ctx = kv[:tc].astype(BF16)
            eye = jnp.pad(jnp.eye(qk_rope, dtype=F32), ((0, kvw - kv_lora - qk_rope), (0, 0)))
            w_k = jnp.concatenate([
                jnp.concatenate([mla_w_uk[j], jnp.zeros((kvw - kv_lora, mla_heads, qk_nope), F32)], axis=0),
                jnp.broadcast_to(jnp.concatenate([jnp.zeros((kv_lora, qk_rope), F32), eye], axis=0)[:, None, :],
                                 (kvw, mla_heads, qk_rope)),
                jnp.zeros((kvw, mla_heads, hq - qk_dim), F32)], axis=-1).reshape(kvw, mla_heads * hq).astype(BF16)
            w_v = _pad_rows(mla_w_uv[j].reshape(kv_lora, mla_heads * v_dim), kvw).astype(BF16)
            k_ctx = _matmul(ckv_ctx, [w_k], out_dtype=BF16, tm=1024, tn=1024, tk=kvw, name="mla_k_ctx")
            v_ctx = _matmul(ckv_ctx, [w_v], out_dtype=BF16, tm=1024, tn=1024, tk=kvw, name="mla_v_ctx")
            k_lat = _matmul(ckv_all, [w_k], out_dtype=BF16, tm=1024, tn=1024, tk=kvw, name="mla_k_lat")
            v_lat = _matmul(ckv_all, [w_v], out_dtype=BF16, tm=1024, tn=1024, tk=kvw, name="mla_v_lat")
            o_ctx = _attention(q, k_ctx, v_ctx, nb=nbc, nh=mla_heads, sq=seq, sk=seq, dk=hq, dv=v_dim,
                               scale=None, heads=CTX_HEADS, out_rows=t, name="mla_context")
            o = _attention(q, k_lat, v_lat, nb=nbl, nh=mla_heads, sq=n_lat, sk=n_lat + past, dk=hq, dv=v_dim,
                           scale=None, q_row0=tc, out_rows=t, out_row0=tc, base=o_ctx, name="mla_latent")
            w_o = _cast_weights(mla_w_o, j, 1)
        else:
            o = _fourier_mix(h, [(0, nbc, seq), (tc, nbl, n_lat)], fnet_groups)
            w_o = _cast_weights(fnet_w_o, j, 1)
        y = _matmul(o, [w_o], out_dtype=F32, tm=1024, tn=1024, tk=4096, name="mixer_out")

        jc = l // 2
        moe = l % 2 == 1
        x, h = _residual_ln(x, [y], mod[l, 2], ln_g[l, 0], ln_b[l, 0], (mod[l, 3], mod[l, 4]), tile_group, tm_tok,
                            alpha, h_dtype=F32 if moe else BF16)
        nxt = (mod[l + 1, 0], mod[l + 1, 1]) if l + 1 < depth else None
        if not moe:
            dff = _round_up(ffn_w_gate.shape[-1], 1024)
            u = _matmul(h, [_cast_weights(ffn_w_gate, jc, 1, cols_out=dff), _cast_weights(ffn_w_up, jc, 1, cols_out=dff)],
                        out_dtype=BF16, tm=1024, tn=512, tk=4096, epilogue=_swiglu_epilogue, name="ffn_up")
            y = _matmul(u, [_cast_weights(ffn_w_down, jc, 1, rows_out=dff)], out_dtype=F32, tm=1024, tn=1024, tk=2816,
                        name="ffn_down")
            x, h = _residual_ln(x, [y], mod[l, 5], ln_g[l, 1], ln_b[l, 1], nxt, tile_group, tm_tok, alpha)
        else:
            tm_e = 512
            dfe = _round_up(moe_w_gate.shape[-1], 512)
            idx, gates = _router(h, moe_w_router[jc], moe_b_router[jc], tm_tok)
            src_token, slot_pos, tile_expert, tile_valid = _dispatch_plan(idx[:, :TOP_K], n_experts, tm_e)
            hs = _gather_rows(h, src_token, 256)
            group = (tile_expert, tile_valid)
            u = _matmul(hs, [_cast_weights(moe_w_gate.reshape((-1,) + moe_w_gate.shape[2:]), jc * n_experts, n_experts, cols_out=dfe),
                         _cast_weights(moe_w_up.reshape((-1,) + moe_w_up.shape[2:]), jc * n_experts, n_experts, cols_out=dfe)],
                        out_dtype=BF16, tm=tm_e, tn=512, tk=4096, epilogue=_swiglu_epilogue, group=group,
                        name="moe_up")
            ys = _matmul(u, [_cast_weights(moe_w_down.reshape((-1,) + moe_w_down.shape[2:]), jc * n_experts, n_experts, rows_out=dfe)], out_dtype=F32, tm=tm_e, tn=1024, tk=dfe,
                         group=group, name="moe_down")
            y0 = _gather_rows(ys, slot_pos[:, 0], 256)
            y1 = _gather_rows(ys, slot_pos[:, 1], 256)
            x, h = _residual_ln(x, [y0, y1], mod[l, 5], ln_g[l, 1], ln_b[l, 1], nxt, tile_group, tm_tok, alpha,
                                gates=gates)

    y_prompt = x[:tc].reshape(nbc, seq, d)
    y_sample = x[tc:].reshape(nbl, n_lat, d)
    return (y_prompt, y_sample, jnp.stack(na_k, axis=1), jnp.stack(na_v, axis=1),
            jnp.stack(mla_ckv, axis=1), jnp.stack(mla_kpe, axis=1))
```
